```python
import jax
import jax.numpy as jnp
from jax import lax
import numpy as np

D_MODEL = 2048
BATCH = 2
SEQ = 4096
DEPTH = 2

GRID_W = 64
CTX_LEN = 256
N_EVEN = (DEPTH + 1) // 2
N_ODD = DEPTH // 2
EPS = 1e-6
N_MOD = 6

CHUNK = 128
A_HEADS = 8
A_HEAD_DIM = 128
A_WIDTH = A_HEADS * A_HEAD_DIM
B_WIDTH = 1024
B_CONV = 31
MIX_IN = 2 * A_WIDTH + 2 * B_WIDTH
MIX_OUT = A_WIDTH + B_WIDTH

MLA_HEADS = 16
Q_LORA = 768
KV_LORA = 512
QK_NOPE = 128
QK_ROPE = 64
V_DIM = 128
ROPE_THETA = 10000.0
Q_BLOCK = 128
MLA_IN = Q_LORA + KV_LORA + QK_ROPE

D_FF = 5632
FFN_CONV = 3

kernel_name = 'hybrid_gmlp_conformer_mla_dit_block'


def rmsnorm(x, g):
    xf = x.astype(jnp.float32)
    y = xf * lax.rsqrt(jnp.mean(xf * xf, axis=-1, keepdims=True) + EPS)
    return (y * g.astype(jnp.float32)).astype(x.dtype)


def layernorm(x, g, b):
    xf = x.astype(jnp.float32)
    mu = jnp.mean(xf, axis=-1, keepdims=True)
    var = jnp.mean(jnp.square(xf - mu), axis=-1, keepdims=True)
    y = (xf - mu) * lax.rsqrt(var + EPS)
    return (y * g.astype(jnp.float32) + b.astype(jnp.float32)).astype(x.dtype)


def modulate(h, shift, scale):
    return h * (1 + scale) + shift


def adaln(cvec, w, b):
    m = jax.nn.silu(cvec) @ w + b
    return jnp.split(m, N_MOD, axis=-1)


def depthwise_conv(x, w, b):
    pad = (w.shape[0] - 1) // 2
    y = lax.conv_general_dilated(x, w[:, None, :].astype(x.dtype), window_strides=(1,), padding=[(pad, pad)], dimension_numbers=('NWC', 'WIO', 'NWC'), feature_group_count=x.shape[-1])
    return y + b.astype(x.dtype)


def chunk_gmlp(z, ln_g, ln_b, w_s, b_s):
    z = jax.nn.gelu(z)
    u, v = z[..., :A_WIDTH], z[..., A_WIDTH:]
    v = layernorm(v, ln_g, ln_b)
    bn, L, _ = v.shape
    v = v.reshape(bn, L // CHUNK, CHUNK, A_HEADS, A_HEAD_DIM)
    v = jnp.einsum('hij,bnjhd->bnihd', w_s.astype(v.dtype), v) + b_s.T.astype(v.dtype)[:, :, None]
    return u * v.reshape(bn, L, A_WIDTH)


def conformer_conv(z, conv_w, conv_b, ln_g, ln_b):
    a, g = z[..., :B_WIDTH], z[..., B_WIDTH:]
    h = a * jax.nn.sigmoid(g)
    h = depthwise_conv(h, conv_w, conv_b)
    h = layernorm(h, ln_g, ln_b)
    return jax.nn.silu(h)


def ab_mixer(h, w_in, b_in, a_ln_g, a_ln_b, a_w_s, a_b_s, b_conv_w, b_conv_b, b_ln_g, b_ln_b, w_out):
    z = h @ w_in + b_in
    ya = chunk_gmlp(z[..., :2 * A_WIDTH], a_ln_g, a_ln_b, a_w_s, a_b_s)
    yb = conformer_conv(z[..., 2 * A_WIDTH:], b_conv_w, b_conv_b, b_ln_g, b_ln_b)
    return jnp.concatenate([ya, yb], axis=-1) @ w_out


def conv_ffn(h, w_up, conv_w, conv_b, w_down):
    z = h @ w_up
    g, u = z[..., :D_FF], z[..., D_FF:]
    g = depthwise_conv(g, conv_w, conv_b)
    return (jax.nn.silu(g) * u) @ w_down


def axial_rope(L):
    rows = L // GRID_W
    row = jnp.repeat(jnp.arange(rows, dtype=jnp.float32), GRID_W)
    col = jnp.tile(jnp.arange(GRID_W, dtype=jnp.float32), rows)
    n_freq = QK_ROPE // 4
    inv = ROPE_THETA ** (-jnp.arange(n_freq, dtype=jnp.float32) / n_freq)
    ang = jnp.concatenate([row[:, None] * inv, col[:, None] * inv], axis=-1)
    return jnp.cos(ang), jnp.sin(ang)


def apply_rope(x, cos, sin):
    half = x.shape[-1] // 2
    x1, x2 = x[..., :half], x[..., half:]
    cos = cos.astype(x.dtype)
    sin = sin.astype(x.dtype)
    return jnp.concatenate([x1 * cos - x2 * sin, x2 * cos + x1 * sin], axis=-1)


def mla_queries(cq, q_norm_g, w_uq, cos, sin):
    bn, L, _ = cq.shape
    q = (rmsnorm(cq, q_norm_g) @ w_uq).reshape(bn, L, MLA_HEADS, QK_NOPE + QK_ROPE)
    q_nope, q_pe = q[..., :QK_NOPE], q[..., QK_NOPE:]
    if cos is not None:
        q_pe = apply_rope(q_pe, cos[None, :, None, :], sin[None, :, None, :])
    return q_nope, q_pe


def mla_keys_values(ckv, k_pe, kv_norm_g, w_ukv, cos, sin):
    bn, L, _ = ckv.shape
    kv = (rmsnorm(ckv, kv_norm_g) @ w_ukv).reshape(bn, L, MLA_HEADS, QK_NOPE + V_DIM)
    k_nope, v = kv[..., :QK_NOPE], kv[..., QK_NOPE:]
    if cos is not None:
        k_pe = apply_rope(k_pe, cos[None], sin[None])
    return k_nope, k_pe, v


def attend(q_nope, q_pe, k_nope, k_pe, v):
    bn, lq, _, _ = q_nope.shape
    nb = lq // Q_BLOCK
    qn = q_nope.reshape(bn, nb, Q_BLOCK, MLA_HEADS, QK_NOPE).transpose(1, 0, 2, 3, 4)
    qp = q_pe.reshape(bn, nb, Q_BLOCK, MLA_HEADS, QK_ROPE).transpose(1, 0, 2, 3, 4)
    scale = (QK_NOPE + QK_ROPE) ** -0.5

    def block(args):
        qn_b, qp_b = args
        s = jnp.einsum('bqhd,bkhd->bhqk', qn_b, k_nope) + jnp.einsum('bqhd,bkd->bhqk', qp_b, k_pe)
        p = jax.nn.softmax(s.astype(jnp.float32) * scale, axis=-1).astype(v.dtype)
        return jnp.einsum('bhqk,bkhd->bqhd', p, v)

    o = lax.map(block, (qn, qp))
    return o.transpose(1, 0, 2, 3, 4).reshape(bn, lq, MLA_HEADS * V_DIM)


def mla_mixer(h_lat, h_ctx, w_in, q_norm_g, w_uq, kv_norm_g, w_ukv, w_o, cos, sin, ctx_out):
    z = h_lat @ w_in
    cq, ckv, kpe = z[..., :Q_LORA], z[..., Q_LORA:Q_LORA + KV_LORA], z[..., Q_LORA + KV_LORA:]
    zc = h_ctx @ w_in[:, Q_LORA:]
    ckv_c, kpe_c = zc[..., :KV_LORA], zc[..., KV_LORA:]
    kn_l, kp_l, v_l = mla_keys_values(ckv, kpe, kv_norm_g, w_ukv, cos, sin)
    kn_c, kp_c, v_c = mla_keys_values(ckv_c, kpe_c, kv_norm_g, w_ukv, None, None)
    qn, qp = mla_queries(cq, q_norm_g, w_uq, cos, sin)
    o_lat = attend(qn, qp, jnp.concatenate([kn_l, kn_c], axis=1), jnp.concatenate([kp_l, kp_c], axis=1), jnp.concatenate([v_l, v_c], axis=1)) @ w_o
    o_ctx = None
    if ctx_out:
        qn_c, qp_c = mla_queries(h_ctx @ w_in[:, :Q_LORA], q_norm_g, w_uq, None, None)
        o_ctx = attend(qn_c, qp_c, kn_c, kp_c, v_c) @ w_o
    return o_lat, o_ctx


def setup_inputs(seed: int = 0) -> dict:
    key = jax.random.key(seed)
    ks = iter(jax.random.split(key, 40))
    f32 = jnp.float32

    def nrm(shape, scale):
        return jax.random.normal(next(ks), shape, f32) * scale

    def gain(shape):
        return 1.0 + nrm(shape, 0.02)

    return {
        'x': nrm((BATCH, SEQ, D_MODEL), 1.0),
        'c': nrm((BATCH, D_MODEL), 1.0),
        'ctx': nrm((BATCH, CTX_LEN, D_MODEL), 1.0),
        'c_ctx': nrm((D_MODEL,), 1.0),
        'norm1_g': gain((DEPTH, D_MODEL)),
        'norm2_g': gain((DEPTH, D_MODEL)),
        'w_ada': nrm((DEPTH, D_MODEL, N_MOD * D_MODEL), D_MODEL ** -0.5),
        'b_ada': nrm((DEPTH, N_MOD * D_MODEL), 0.02),
        'ab_w_in': nrm((N_EVEN, D_MODEL, MIX_IN), D_MODEL ** -0.5),
        'ab_b_in': nrm((N_EVEN, MIX_IN), 0.02),
        'a_ln_g': gain((N_EVEN, A_WIDTH)),
        'a_ln_b': nrm((N_EVEN, A_WIDTH), 0.02),
        'a_w_s': nrm((N_EVEN, A_HEADS, CHUNK, CHUNK), CHUNK ** -0.5),
        'a_b_s': gain((N_EVEN, A_HEADS, CHUNK)),
        'b_conv_w': nrm((N_EVEN, B_CONV, B_WIDTH), B_CONV ** -0.5),
        'b_conv_b': nrm((N_EVEN, B_WIDTH), 0.02),
        'b_ln_g': gain((N_EVEN, B_WIDTH)),
        'b_ln_b': nrm((N_EVEN, B_WIDTH), 0.02),
        'ab_w_out': nrm((N_EVEN, MIX_OUT, D_MODEL), MIX_OUT ** -0.5),
        'mla_w_in': nrm((N_ODD, D_MODEL, MLA_IN), D_MODEL ** -0.5),
        'mla_q_norm_g': gain((N_ODD, Q_LORA)),
        'mla_w_uq': nrm((N_ODD, Q_LORA, MLA_HEADS * (QK_NOPE + QK_ROPE)), Q_LORA ** -0.5),
        'mla_kv_norm_g': gain((N_ODD, KV_LORA)),
        'mla_w_ukv': nrm((N_ODD, KV_LORA, MLA_HEADS * (QK_NOPE + V_DIM)), KV_LORA ** -0.5),
        'mla_w_o': nrm((N_ODD, MLA_HEADS * V_DIM, D_MODEL), (MLA_HEADS * V_DIM) ** -0.5),
        'ffn_w_up': nrm((DEPTH, D_MODEL, 2 * D_FF), D_MODEL ** -0.5),
        'ffn_conv_w': nrm((DEPTH, FFN_CONV, D_FF), FFN_CONV ** -0.5),
        'ffn_conv_b': nrm((DEPTH, D_FF), 0.02),
        'ffn_w_down': nrm((DEPTH, D_FF, D_MODEL), D_FF ** -0.5),
        'final_norm_g': gain((D_MODEL,)),
    }


def reference(x, c, ctx, c_ctx, norm1_g, norm2_g, w_ada, b_ada, ab_w_in, ab_b_in, a_ln_g, a_ln_b, a_w_s, a_b_s, b_conv_w, b_conv_b, b_ln_g, b_ln_b, ab_w_out, mla_w_in, mla_q_norm_g, mla_w_uq, mla_kv_norm_g, mla_w_ukv, mla_w_o, ffn_w_up, ffn_conv_w, ffn_conv_b, ffn_w_down, final_norm_g):
    L = x.shape[1]
    cos, sin = axial_rope(L)
    xl, xc = x, ctx
    for i in range(DEPTH):
        last = i == DEPTH - 1
        even = i % 2 == 0
        j = i // 2
        ctx_in = (not last) or (not even)
        ctx_update = not last
        sh1, sc1, g1, sh2, sc2, g2 = [m[:, None, :] for m in adaln(c, w_ada[i], b_ada[i])]
        if ctx_in:
            csh1, csc1, cg1, csh2, csc2, cg2 = adaln(c_ctx, w_ada[i], b_ada[i])
        hl = modulate(rmsnorm(xl, norm1_g[i]), sh1, sc1)
        if even:
            ab_args = (ab_w_in[j], ab_b_in[j], a_ln_g[j], a_ln_b[j], a_w_s[j], a_b_s[j], b_conv_w[j], b_conv_b[j], b_ln_g[j], b_ln_b[j], ab_w_out[j])
            yl = ab_mixer(hl, *ab_args)
            if ctx_update:
                hc = modulate(rmsnorm(xc, norm1_g[i]), csh1, csc1)
                xc = xc + cg1 * ab_mixer(hc, *ab_args)
        else:
            hc = modulate(rmsnorm(xc, norm1_g[i]), csh1, csc1)
            yl, yc = mla_mixer(hl, hc, mla_w_in[j], mla_q_norm_g[j], mla_w_uq[j], mla_kv_norm_g[j], mla_w_ukv[j], mla_w_o[j], cos, sin, ctx_update)
            if ctx_update:
                xc = xc + cg1 * yc
        xl = xl + g1 * yl
        xl = xl + g2 * conv_ffn(modulate(rmsnorm(xl, norm2_g[i]), sh2, sc2), ffn_w_up[i], ffn_conv_w[i], ffn_conv_b[i], ffn_w_down[i])
        if ctx_update:
            xc = xc + cg2 * conv_ffn(modulate(rmsnorm(xc, norm2_g[i]), csh2, csc2), ffn_w_up[i], ffn_conv_w[i], ffn_conv_b[i], ffn_w_down[i])
    return rmsnorm(xl, final_norm_g)
```

```python
import functools

import jax
import jax.numpy as jnp
from jax import lax
from jax.experimental import pallas as pl
from jax.experimental.pallas import tpu as pltpu

F32 = jnp.float32
BF16 = jnp.bfloat16

D_MODEL = 2048
GRID_W = 64
EPS = 1e-6
N_MOD = 6
CHUNK = 128
A_HEADS = 8
A_WIDTH = 1024
B_WIDTH = 1024
B_CONV = 31
MLA_HEADS = 16
Q_LORA = 768
KV_LORA = 512
QK_NOPE = 128
QK_ROPE = 64
V_DIM = 128
ROPE_THETA = 10000.0
D_FF = 5632

LANES = 128
BF16_SUBLANES = 16
HEAD_PAD = 256
VMEM_LIMIT = 56 * 2**20

HALO = BF16_SUBLANES
B_PAD = (B_CONV - 1) // 2


def _cparams(*sem):
    return pltpu.CompilerParams(dimension_semantics=sem, vmem_limit_bytes=VMEM_LIMIT)


def _const_spec(shape):
    n = len(shape)
    return pl.BlockSpec(shape, lambda *_: (0,) * n, pipeline_mode=pl.Buffered(1))


def _rms(x, g):
    return x * lax.rsqrt(jnp.mean(x * x, axis=-1, keepdims=True) + EPS) * g


def _layernorm(x, g, b):
    mu = jnp.mean(x, axis=-1, keepdims=True)
    xc = x - mu
    var = jnp.mean(xc * xc, axis=-1, keepdims=True)
    return xc * lax.rsqrt(var + EPS) * g + b


def _norm_mod(x, g, shift, scale):
    return _rms(x, g) * (1.0 + scale) + shift


def _dot(a, b):
    return jnp.dot(a, b, preferred_element_type=F32)


def _adaln_kernel(cv_ref, w_ref, b_ref, o_ref):
    cv = cv_ref[...]
    a = (cv * jax.nn.sigmoid(cv)).astype(BF16)
    o_ref[...] = _dot(a, w_ref[...].astype(BF16)) + b_ref[...]


def _adaln(cv, w_ada, b_ada):
    depth, d, n = w_ada.shape
    tn = 1024
    return pl.pallas_call(
        _adaln_kernel,
        grid=(depth, n // tn),
        in_specs=[
            pl.BlockSpec((8, d), lambda l, j: (0, 0)),
            pl.BlockSpec((None, d, tn), lambda l, j: (l, 0, j)),
            pl.BlockSpec((None, 1, tn), lambda l, j: (l, 0, j)),
        ],
        out_specs=pl.BlockSpec((None, 8, tn), lambda l, j: (l, 0, j)),
        out_shape=jax.ShapeDtypeStruct((depth, 8, n), F32),
        compiler_params=_cparams("parallel", "arbitrary"),
        name="adaln",
    )(cv, w_ada, b_ada.reshape(depth, 1, n))


def _nmm_kernel(x_ref, g_ref, sh_ref, sc_ref, w_ref, b_ref, o_ref, h_ref):
    @pl.when(pl.program_id(1) == 0)
    def _():
        h_ref[...] = _norm_mod(x_ref[...], g_ref[...], sh_ref[...], sc_ref[...]).astype(BF16)

    o_ref[...] = _dot(h_ref[...], w_ref[...]) + b_ref[...]


def _nmm(x, g, shift, scale, w, b, *, tm, tn, tiles_per_batch):
    m, d = x.shape
    n = w.shape[1]
    vec = pl.BlockSpec((None, 1, d), lambda i, j: (i // tiles_per_batch, 0, 0))
    return pl.pallas_call(
        _nmm_kernel,
        grid=(m // tm, n // tn),
        in_specs=[
            pl.BlockSpec((tm, d), lambda i, j: (i, 0)),
            pl.BlockSpec((1, d), lambda i, j: (0, 0)),
            vec, vec,
            pl.BlockSpec((d, tn), lambda i, j: (0, j)),
            pl.BlockSpec((1, tn), lambda i, j: (0, j)),
        ],
        out_specs=pl.BlockSpec((tm, tn), lambda i, j: (i, j)),
        out_shape=jax.ShapeDtypeStruct((m, n), F32),
        scratch_shapes=[pltpu.VMEM((tm, d), BF16)],
        compiler_params=_cparams("parallel", "arbitrary"),
        name="norm_mod_matmul",
    )(x, g, shift, scale, w, b)


_CONV_LANES = 256
_CONV_ROWS = 32
_S_EXTRA = 24


def _abmix_kernel(z_ref, zp_ref, zn_ref, x_ref, gate_ref, alng_ref, alnb_ref, ws_ref, bs_ref,
                  cw_ref, cb_ref, blng_ref, blnb_ref, wout_ref, o_ref,
                  e_ref, s_ref, hc_ref, y_ref, *, tm, tiles_per_seg):
    i = pl.program_id(0)
    pos = i % tiles_per_seg
    nch = tm // CHUNK

    ga = jax.nn.gelu(z_ref[:, :2 * A_WIDTH], approximate=True)
    u = ga[:, :A_WIDTH]
    vb = _layernorm(ga[:, A_WIDTH:], alng_ref[...], alnb_ref[...]).astype(BF16)
    for h in range(A_HEADS):
        cols = slice(h * CHUNK, (h + 1) * CHUNK)
        rhs = jnp.concatenate([vb[c * CHUNK:(c + 1) * CHUNK, cols] for c in range(nch)], axis=1)
        r = _dot(ws_ref[h], rhs)
        for c in range(nch):
            rows = slice(c * CHUNK, (c + 1) * CHUNK)
            mixed = r[:, c * CHUNK:(c + 1) * CHUNK] + bs_ref[:, cols]
            y_ref[rows, cols] = (u[rows, cols] * mixed).astype(BF16)

    def glu(zz):
        return zz[:, :B_WIDTH] * jax.nn.sigmoid(zz[:, B_WIDTH:])

    e_ref[0:HALO, :] = jnp.where(pos == 0, 0.0, glu(zp_ref[...]))
    e_ref[HALO:HALO + tm, :] = glu(z_ref[:, 2 * A_WIDTH:])
    e_ref[HALO + tm:, :] = jnp.where(pos == tiles_per_seg - 1, 0.0, glu(zn_ref[...]))

    base = HALO - B_PAD
    for cb in range(B_WIDTH // _CONV_LANES):
        lanes = slice(cb * _CONV_LANES, (cb + 1) * _CONV_LANES)
        for r in range(8):
            s_ref[r] = e_ref[pl.ds(r, tm + _S_EXTRA), lanes]

        def row_block(rb, carry, lanes=lanes):
            r0 = pl.multiple_of(rb * _CONV_ROWS, _CONV_ROWS)
            acc = jnp.zeros((_CONV_ROWS, _CONV_LANES), F32)
            for k in range(B_CONV):
                off = base + k
                tap = s_ref[off % 8, pl.ds(r0 + 8 * (off // 8), _CONV_ROWS), :]
                acc = acc + tap * cw_ref[k:k + 1, lanes]
            hc_ref[pl.ds(r0, _CONV_ROWS), lanes] = acc + cb_ref[:, lanes]
            return carry

        lax.fori_loop(0, tm // _CONV_ROWS, row_block, 0)

    hb = _layernorm(hc_ref[...], blng_ref[...], blnb_ref[...])
    y_ref[:, A_WIDTH:] = (hb * jax.nn.sigmoid(hb)).astype(BF16)

    o_ref[...] = x_ref[...] + gate_ref[...] * _dot(y_ref[...], wout_ref[...])


def _abmix(z, x, gate, p, *, tm, seg_len, tiles_per_batch):
    m, d = x.shape
    tiles_per_seg = seg_len // tm
    hb = tm // HALO
    nhalo = m // HALO
    kern = functools.partial(_abmix_kernel, tm=tm, tiles_per_seg=tiles_per_seg)
    row = lambda w: pl.BlockSpec((1, w), lambda i: (0, 0))
    return pl.pallas_call(
        kern,
        grid=(m // tm,),
        in_specs=[
            pl.BlockSpec((tm, 2 * A_WIDTH + 2 * B_WIDTH), lambda i: (i, 0)),
            pl.BlockSpec((HALO, 2 * B_WIDTH), lambda i: (jnp.maximum(i * hb - 1, 0), 1)),
            pl.BlockSpec((HALO, 2 * B_WIDTH), lambda i: (jnp.minimum((i + 1) * hb, nhalo - 1), 1)),
            pl.BlockSpec((tm, d), lambda i: (i, 0)),
            pl.BlockSpec((None, 1, d), lambda i: (i // tiles_per_batch, 0, 0)),
            row(A_WIDTH), row(A_WIDTH),
            _const_spec((A_HEADS, CHUNK, CHUNK)),
            _const_spec((CHUNK, A_WIDTH)),
            _const_spec((B_CONV, B_WIDTH)),
            row(B_WIDTH), row(B_WIDTH), row(B_WIDTH),
            _const_spec((A_WIDTH + B_WIDTH, d)),
        ],
        out_specs=pl.BlockSpec((tm, d), lambda i: (i, 0)),
        out_shape=jax.ShapeDtypeStruct((m, d), F32),
        scratch_shapes=[
            pltpu.VMEM((tm + 2 * HALO, B_WIDTH), F32),
            pltpu.VMEM((8, tm + _S_EXTRA, _CONV_LANES), F32),
            pltpu.VMEM((tm, B_WIDTH), F32),
            pltpu.VMEM((tm, A_WIDTH + B_WIDTH), BF16),
        ],
        compiler_params=_cparams("parallel"),
        name="ab_mixer",
    )(z, z, z, x, gate, p["a_ln_g"], p["a_ln_b"], p["w_s"], p["b_s_full"], p["conv_w"], p["conv_b"],
      p["b_ln_g"], p["b_ln_b"], p["w_out"])


def _ffn_kernel(x_ref, xp_ref, xn_ref, ng_ref, sh_ref, sc_ref, gate_ref, wg_ref, wu_ref, cw_ref, cb_ref,
                wd_ref, fg_ref, o_ref, h_ref, g_ref, *, tm, tiles_per_seg, nf, final_norm):
    i = pl.program_id(0)
    f = pl.program_id(1)
    pos = i % tiles_per_seg

    @pl.when(f == 0)
    def _():
        def hmod(xx):
            return _norm_mod(xx, ng_ref[...], sh_ref[...], sc_ref[...])

        h_ref[0:HALO, :] = jnp.where(pos == 0, 0.0, hmod(xp_ref[...])).astype(BF16)
        h_ref[HALO:HALO + tm, :] = hmod(x_ref[...]).astype(BF16)
        h_ref[HALO + tm:, :] = jnp.where(pos == tiles_per_seg - 1, 0.0, hmod(xn_ref[...])).astype(BF16)

    g_ref[...] = _dot(h_ref[...], wg_ref[...])
    u = _dot(h_ref[HALO:HALO + tm, :], wu_ref[...])
    gc = (cw_ref[0:1, :] * g_ref[pl.ds(HALO - 1, tm), :] + cw_ref[1:2, :] * g_ref[pl.ds(HALO, tm), :]
          + cw_ref[2:3, :] * g_ref[pl.ds(HALO + 1, tm), :] + cb_ref[...])
    a = (gc * jax.nn.sigmoid(gc) * u).astype(BF16)
    contrib = _dot(a, wd_ref[...])

    @pl.when(f == 0)
    def _():
        o_ref[...] = contrib

    @pl.when(f > 0)
    def _():
        o_ref[...] += contrib

    @pl.when(f == nf - 1)
    def _():
        y = x_ref[...] + gate_ref[...] * o_ref[...]
        if final_norm:
            y = _rms(y, fg_ref[...])
        o_ref[...] = y


def _ffn(x, ng, shift, scale, gate, w_up, conv_w, conv_b, w_down, fg, *, tm, tf, seg_len, tiles_per_batch,
         final_norm):
    m, d = x.shape
    nf = D_FF // tf
    tiles_per_seg = seg_len // tm
    hb = tm // HALO
    nhalo = m // HALO
    kern = functools.partial(_ffn_kernel, tm=tm, tiles_per_seg=tiles_per_seg, nf=nf, final_norm=final_norm)
    vec = pl.BlockSpec((None, 1, d), lambda i, f: (i // tiles_per_batch, 0, 0))
    row = pl.BlockSpec((1, d), lambda i, f: (0, 0))
    return pl.pallas_call(
        kern,
        grid=(m // tm, nf),
        in_specs=[
            pl.BlockSpec((tm, d), lambda i, f: (i, 0)),
            pl.BlockSpec((HALO, d), lambda i, f: (jnp.maximum(i * hb - 1, 0), 0)),
            pl.BlockSpec((HALO, d), lambda i, f: (jnp.minimum((i + 1) * hb, nhalo - 1), 0)),
            row, vec, vec, vec,
            pl.BlockSpec((d, tf), lambda i, f: (0, f)),
            pl.BlockSpec((d, tf), lambda i, f: (0, nf + f)),
            pl.BlockSpec((3, tf), lambda i, f: (0, f)),
            pl.BlockSpec((1, tf), lambda i, f: (0, f)),
            pl.BlockSpec((tf, d), lambda i, f: (f, 0)),
            row,
        ],
        out_specs=pl.BlockSpec((tm, d), lambda i, f: (i, 0)),
        out_shape=jax.ShapeDtypeStruct((m, d), F32),
        scratch_shapes=[
            pltpu.VMEM((tm + 2 * HALO, d), BF16),
            pltpu.VMEM((tm + 2 * HALO, tf), F32),
        ],
        compiler_params=_cparams("parallel", "arbitrary"),
        name="conv_ffn",
    )(x, x, x, ng, shift, scale, gate, w_up, w_up, conv_w, conv_b, w_down, fg)


_MLA_TM = 256
_ZPAD = Q_LORA + KV_LORA + LANES


def _rope_tile(xt, c, s1, s2, odd):
    if odd:
        return pltpu.roll(xt, 64, 1) * c + pltpu.roll(xt, 32, 1) * s1 + pltpu.roll(xt, 96, 1) * s2
    return xt * c + pltpu.roll(xt, 96, 1) * s1 + pltpu.roll(xt, 32, 1) * s2


def _mla_proj_kernel(xl_ref, xc_ref, ng_ref, sh_ref, sc_ref, win_ref, qg_ref, kvg_ref, wqn_ref, wqp_ref,
                     wuk_ref, wuv_ref, c_ref, s1_ref, s2_ref, q_ref, k_ref, v_ref, *, n_lat_tiles, q_scale):
    t = pl.program_id(1)
    is_ctx = t >= n_lat_tiles
    x = jnp.where(is_ctx, xc_ref[...], xl_ref[...])
    h = _norm_mod(x, ng_ref[...], sh_ref[...], sc_ref[...]).astype(BF16)
    z = _dot(h, win_ref[...])
    c, s1, s2 = c_ref[...], s1_ref[...], s2_ref[...]

    ckvn = _rms(z[:, Q_LORA:Q_LORA + KV_LORA], kvg_ref[...]).astype(BF16)
    kpe = _rope_tile(z[:, Q_LORA + KV_LORA:], c, s1, s2, False).astype(BF16)
    kn = _dot(ckvn, wuk_ref[...])
    v_ref[...] = _dot(ckvn, wuv_ref[...]).astype(BF16)
    for hd in range(MLA_HEADS):
        k_ref[:, hd * HEAD_PAD:hd * HEAD_PAD + QK_NOPE] = kn[:, hd * QK_NOPE:(hd + 1) * QK_NOPE].astype(BF16)
        k_ref[:, hd * HEAD_PAD + QK_NOPE:(hd + 1) * HEAD_PAD] = kpe

    @pl.when(jnp.logical_not(is_ctx))
    def _():
        cqn = _rms(z[:, :Q_LORA], qg_ref[...]).astype(BF16)
        qn = _dot(cqn, wqn_ref[...])
        qp = _dot(cqn, wqp_ref[...])
        for hd in range(MLA_HEADS):
            q_ref[:, hd * HEAD_PAD:hd * HEAD_PAD + QK_NOPE] = (
                qn[:, hd * QK_NOPE:(hd + 1) * QK_NOPE] * q_scale).astype(BF16)
            xt = qp[:, (hd // 2) * LANES:(hd // 2 + 1) * LANES]
            q_ref[:, hd * HEAD_PAD + QK_NOPE:(hd + 1) * HEAD_PAD] = (
                _rope_tile(xt, c, s1, s2, hd % 2 == 1) * q_scale).astype(BF16)


def _mla_proj(xl, xc, ng, shift, scale, p, tabs):
    bsz, seq, d = xl.shape
    lc = xc.shape[1]
    tm = _MLA_TM
    assert lc == tm
    nlt = seq // tm
    nt = nlt + 1
    kern = functools.partial(_mla_proj_kernel, n_lat_tiles=nlt, q_scale=float((QK_NOPE + QK_ROPE) ** -0.5))
    vec = pl.BlockSpec((None, None, 1, d), lambda b, t: (b, t // nlt, 0, 0))
    tab = pl.BlockSpec((tm, LANES), lambda b, t: (t, 0))
    hw = MLA_HEADS * HEAD_PAD
    return pl.pallas_call(
        kern,
        grid=(bsz, nt),
        in_specs=[
            pl.BlockSpec((None, tm, d), lambda b, t: (b, jnp.minimum(t, nlt - 1), 0)),
            pl.BlockSpec((None, tm, d), lambda b, t: (b, 0, 0)),
            pl.BlockSpec((1, d), lambda b, t: (0, 0)),
            vec, vec,
            _const_spec((d, _ZPAD)),
            pl.BlockSpec((1, Q_LORA), lambda b, t: (0, 0)),
            pl.BlockSpec((1, KV_LORA), lambda b, t: (0, 0)),
            _const_spec((Q_LORA, MLA_HEADS * QK_NOPE)),
            _const_spec((Q_LORA, MLA_HEADS * QK_ROPE)),
            _const_spec((KV_LORA, MLA_HEADS * QK_NOPE)),
            _const_spec((KV_LORA, MLA_HEADS * V_DIM)),
            tab, tab, tab,
        ],
        out_specs=[
            pl.BlockSpec((None, tm, hw), lambda b, t: (b, jnp.minimum(t, nlt - 1), 0)),
            pl.BlockSpec((None, tm, hw), lambda b, t: (b, t, 0)),
            pl.BlockSpec((None, tm, MLA_HEADS * V_DIM), lambda b, t: (b, t, 0)),
        ],
        out_shape=[
            jax.ShapeDtypeStruct((bsz, seq, hw), BF16),
            jax.ShapeDtypeStruct((bsz, seq + lc, hw), BF16),
            jax.ShapeDtypeStruct((bsz, seq + lc, MLA_HEADS * V_DIM), BF16),
        ],
        compiler_params=_cparams("parallel", "arbitrary"),
        name="mla_proj",
    )(xl, xc, ng, shift, scale, p["w_in"], p["q_norm_g"], p["kv_norm_g"], p["w_uq_n"], p["w_uq_p"],
      p["w_uk"], p["w_uv"], *tabs)


def _attn_kernel(q_ref, k_ref, v_ref, o_ref):
    s = lax.dot_general(q_ref[...], k_ref[...], (((1,), (1,)), ((), ())), preferred_element_type=F32)
    p = jnp.exp(s - jnp.max(s, axis=-1, keepdims=True))
    l = jnp.sum(p, axis=-1, keepdims=True)
    o_ref[...] = (_dot(p.astype(BF16), v_ref[...]) / l).astype(o_ref.dtype)


def _attention(q, k, v, *, tq):
    bsz, lq, _ = q.shape
    lk = k.shape[1]
    return pl.pallas_call(
        _attn_kernel,
        grid=(bsz, MLA_HEADS, lq // tq),
        in_specs=[
            pl.BlockSpec((None, tq, HEAD_PAD), lambda b, h, i: (b, i, h)),
            pl.BlockSpec((None, lk, HEAD_PAD), lambda b, h, i: (b, 0, h)),
            pl.BlockSpec((None, lk, V_DIM), lambda b, h, i: (b, 0, h)),
        ],
        out_specs=pl.BlockSpec((None, tq, V_DIM), lambda b, h, i: (b, i, h)),
        out_shape=jax.ShapeDtypeStruct((bsz, lq, MLA_HEADS * V_DIM), BF16),
        compiler_params=_cparams("parallel", "parallel", "arbitrary"),
        name="mla_attention",
    )(q, k, v)


def _oproj_kernel(a_ref, x_ref, gate_ref, w_ref, o_ref):
    o_ref[...] = x_ref[...] + gate_ref[...] * _dot(a_ref[...], w_ref[...])


def _oproj(a, x, gate, w, *, tm, tiles_per_batch):
    m, d = x.shape
    kdim = a.shape[1]
    return pl.pallas_call(
        _oproj_kernel,
        grid=(m // tm,),
        in_specs=[
            pl.BlockSpec((tm, kdim), lambda i: (i, 0)),
            pl.BlockSpec((tm, d), lambda i: (i, 0)),
            pl.BlockSpec((None, 1, d), lambda i: (i // tiles_per_batch, 0, 0)),
            _const_spec((kdim, d)),
        ],
        out_specs=pl.BlockSpec((tm, d), lambda i: (i, 0)),
        out_shape=jax.ShapeDtypeStruct((m, d), F32),
        compiler_params=_cparams("parallel"),
        name="attn_out_proj",
    )(a, x, gate, w)


def _rope_tables(seq, ctx_len):
    rows = seq // GRID_W
    row = jnp.repeat(jnp.arange(rows, dtype=F32), GRID_W)
    col = jnp.tile(jnp.arange(GRID_W, dtype=F32), rows)
    n_freq = QK_ROPE // 4
    inv = ROPE_THETA ** (-jnp.arange(n_freq, dtype=F32) / n_freq)
    ang = jnp.concatenate([row[:, None] * inv, col[:, None] * inv], axis=-1)
    cos, sin = jnp.cos(ang), jnp.sin(ang)
    half = QK_ROPE // 2
    cos = jnp.concatenate([cos, jnp.ones((ctx_len, half), F32)], axis=0)
    sin = jnp.concatenate([sin, jnp.zeros((ctx_len, half), F32)], axis=0)
    zh = jnp.zeros_like(cos)
    c = jnp.concatenate([cos, cos, zh, zh], axis=-1)
    s1 = jnp.concatenate([-sin, zh, zh, zh], axis=-1)
    s2 = jnp.concatenate([zh, sin, zh, zh], axis=-1)
    return c, s1, s2


def kernel(x, c, ctx, c_ctx, norm1_g, norm2_g, w_ada, b_ada, ab_w_in, ab_b_in, a_ln_g, a_ln_b, a_w_s, a_b_s, b_conv_w, b_conv_b, b_ln_g, b_ln_b, ab_w_out, mla_w_in, mla_q_norm_g, mla_w_uq, mla_kv_norm_g, mla_w_ukv, mla_w_o, ffn_w_up, ffn_conv_w, ffn_conv_b, ffn_w_down, final_norm_g):
    bsz, seq, d = x.shape
    lc = ctx.shape[1]
    depth = w_ada.shape[0]

    cv = jnp.concatenate([c, c_ctx[None, :], jnp.zeros((8 - bsz - 1, d), F32)], axis=0)
    mods = _adaln(cv, w_ada, b_ada)

    def lat_mod(i, k):
        return mods[i, :bsz, k * d:(k + 1) * d].reshape(bsz, 1, d)

    def ctx_mod(i, k):
        return mods[i, bsz:bsz + 1, k * d:(k + 1) * d].reshape(1, 1, d)

    xl = x.reshape(bsz * seq, d)
    xc = ctx.reshape(bsz * lc, d)
    n_ctx_rows = bsz * lc

    for i in range(depth):
        last = i == depth - 1
        even = i % 2 == 0
        j = i // 2
        ctx_update = not last
        ng1 = norm1_g[i].reshape(1, d)
        ng2 = norm2_g[i].reshape(1, d)
        if even:
            p = {
                "a_ln_g": a_ln_g[j].reshape(1, -1), "a_ln_b": a_ln_b[j].reshape(1, -1),
                "w_s": a_w_s[j].astype(BF16),
                "b_s_full": jnp.repeat(a_b_s[j].T, A_WIDTH // A_HEADS, axis=1),
                "conv_w": b_conv_w[j], "conv_b": b_conv_b[j].reshape(1, -1),
                "b_ln_g": b_ln_g[j].reshape(1, -1), "b_ln_b": b_ln_b[j].reshape(1, -1),
                "w_out": ab_w_out[j].astype(BF16),
            }
            w_in = ab_w_in[j].astype(BF16)
            b_in = ab_b_in[j].reshape(1, -1)
            z = _nmm(xl, ng1, lat_mod(i, 0), lat_mod(i, 1), w_in, b_in, tm=1024, tn=1024,
                     tiles_per_batch=seq // 1024)
            xl = _abmix(z, xl, lat_mod(i, 2), p, tm=256, seg_len=seq, tiles_per_batch=seq // 256)
            if ctx_update:
                zc = _nmm(xc, ng1, ctx_mod(i, 0), ctx_mod(i, 1), w_in, b_in, tm=n_ctx_rows, tn=1024,
                          tiles_per_batch=1)
                xc = _abmix(zc, xc, ctx_mod(i, 2), p, tm=256, seg_len=lc, tiles_per_batch=n_ctx_rows // 256)
        else:
            w_uq = mla_w_uq[j].reshape(Q_LORA, MLA_HEADS, QK_NOPE + QK_ROPE)
            w_ukv = mla_w_ukv[j].reshape(KV_LORA, MLA_HEADS, QK_NOPE + V_DIM)
            p = {
                "w_in": jnp.pad(mla_w_in[j], ((0, 0), (0, LANES - QK_ROPE))).astype(BF16),
                "q_norm_g": mla_q_norm_g[j].reshape(1, -1), "kv_norm_g": mla_kv_norm_g[j].reshape(1, -1),
                "w_uq_n": w_uq[:, :, :QK_NOPE].reshape(Q_LORA, -1).astype(BF16),
                "w_uq_p": w_uq[:, :, QK_NOPE:].reshape(Q_LORA, -1).astype(BF16),
                "w_uk": w_ukv[:, :, :QK_NOPE].reshape(KV_LORA, -1).astype(BF16),
                "w_uv": w_ukv[:, :, QK_NOPE:].reshape(KV_LORA, -1).astype(BF16),
            }
            shift = jnp.stack([lat_mod(i, 0), jnp.broadcast_to(ctx_mod(i, 0), (bsz, 1, d))], axis=1)
            scale = jnp.stack([lat_mod(i, 1), jnp.broadcast_to(ctx_mod(i, 1), (bsz, 1, d))], axis=1)
            q, k, v = _mla_proj(xl.reshape(bsz, seq, d), xc.reshape(bsz, lc, d), ng1, shift, scale, p,
                                _rope_tables(seq, lc))
            a = _attention(q, k, v, tq=512)
            xl = _oproj(a.reshape(bsz * seq, -1), xl, lat_mod(i, 2), mla_w_o[j].astype(BF16), tm=512,
                        tiles_per_batch=seq // 512)
            assert not ctx_update, "context-side MLA output is only needed when a later layer reads it"
        w_up = ffn_w_up[i].astype(BF16)
        w_down = ffn_w_down[i].astype(BF16)
        cw, cb = ffn_conv_w[i], ffn_conv_b[i].reshape(1, -1)
        fg = final_norm_g.reshape(1, d)
        xl = _ffn(xl, ng2, lat_mod(i, 3), lat_mod(i, 4), lat_mod(i, 5), w_up, cw, cb, w_down, fg,
                  tm=512, tf=512, seg_len=seq, tiles_per_batch=seq // 512, final_norm=last)
        if ctx_update:
            xc = _ffn(xc, ng2, ctx_mod(i, 3), ctx_mod(i, 4), ctx_mod(i, 5), w_up, cw, cb, w_down, fg,
                      tm=256, tf=512, seg_len=lc, tiles_per_batch=n_ctx_rows // 256, final_norm=False)
    return xl.reshape(bsz, seq, d)
```

```python
import functools

import jax
import jax.numpy as jnp
from jax import lax
from jax.experimental import pallas as pl
from jax.experimental.pallas import tpu as pltpu

F32 = jnp.float32
BF16 = jnp.bfloat16

D_MODEL = 2048
GRID_W = 64
EPS = 1e-6
N_MOD = 6
CHUNK = 128
A_HEADS = 8
A_WIDTH = 1024
B_WIDTH = 1024
B_CONV = 31
MLA_HEADS = 16
Q_LORA = 768
KV_LORA = 512
QK_NOPE = 128
QK_ROPE = 64
V_DIM = 128
ROPE_THETA = 10000.0
D_FF = 5632
LOG2_E = 1.4426950408889634

LANES = 128
BF16_SUBLANES = 16
HEAD_PAD = 256
VMEM_LIMIT = 56 * 2**20

HALO = BF16_SUBLANES
B_PAD = (B_CONV - 1) // 2


def _cparams(*sem):
    return pltpu.CompilerParams(dimension_semantics=sem, vmem_limit_bytes=VMEM_LIMIT)


def _const_spec(shape):
    n = len(shape)
    return pl.BlockSpec(shape, lambda *_: (0,) * n, pipeline_mode=pl.Buffered(1))


def _rms(x, g):
    return x * lax.rsqrt(jnp.mean(x * x, axis=-1, keepdims=True) + EPS) * g


def _layernorm(x, g, b):
    mu = jnp.mean(x, axis=-1, keepdims=True)
    xc = x - mu
    var = jnp.mean(xc * xc, axis=-1, keepdims=True)
    return xc * lax.rsqrt(var + EPS) * g + b


def _norm_mod(x, g, shift, scale):
    return _rms(x, g) * (1.0 + scale) + shift


def _dot(a, b):
    return jnp.dot(a, b, preferred_element_type=F32)


def _adaln_kernel(cv_ref, w_ref, b_ref, o_ref):
    cv = cv_ref[...]
    a = (cv * jax.nn.sigmoid(cv)).astype(BF16)
    o_ref[...] = _dot(a, w_ref[...].astype(BF16)) + b_ref[...]


def _adaln(cv, w_ada, b_ada):
    depth, d, n = w_ada.shape
    tn = 1024
    return pl.pallas_call(
        _adaln_kernel,
        grid=(depth, n // tn),
        in_specs=[
            pl.BlockSpec((8, d), lambda l, j: (0, 0)),
            pl.BlockSpec((None, d, tn), lambda l, j: (l, 0, j)),
            pl.BlockSpec((None, 1, tn), lambda l, j: (l, 0, j)),
        ],
        out_specs=pl.BlockSpec((None, 8, tn), lambda l, j: (l, 0, j)),
        out_shape=jax.ShapeDtypeStruct((depth, 8, n), F32),
        compiler_params=_cparams("parallel", "arbitrary"),
        name="adaln",
    )(cv, w_ada, b_ada.reshape(depth, 1, n))


def _nmm_kernel(x_ref, g_ref, sh_ref, sc_ref, w_ref, b_ref, o_ref, h_ref):
    @pl.when(pl.program_id(1) == 0)
    def _():
        h_ref[...] = _norm_mod(x_ref[...], g_ref[...], sh_ref[...], sc_ref[...]).astype(BF16)

    o_ref[...] = _dot(h_ref[...], w_ref[...]) + b_ref[...]


def _nmm(x, g, shift, scale, w, b, *, tm, tn, tiles_per_batch):
    m, d = x.shape
    n = w.shape[1]
    vec = pl.BlockSpec((None, 1, d), lambda i, j: (i // tiles_per_batch, 0, 0))
    return pl.pallas_call(
        _nmm_kernel,
        grid=(m // tm, n // tn),
        in_specs=[
            pl.BlockSpec((tm, d), lambda i, j: (i, 0)),
            pl.BlockSpec((1, d), lambda i, j: (0, 0)),
            vec, vec,
            pl.BlockSpec((d, tn), lambda i, j: (0, j)),
            pl.BlockSpec((1, tn), lambda i, j: (0, j)),
        ],
        out_specs=pl.BlockSpec((tm, tn), lambda i, j: (i, j)),
        out_shape=jax.ShapeDtypeStruct((m, n), F32),
        scratch_shapes=[pltpu.VMEM((tm, d), BF16)],
        compiler_params=_cparams("parallel", "arbitrary"),
        name="norm_mod_matmul",
    )(x, g, shift, scale, w, b)


_CONV_LANES = 256
_CONV_ROWS = 32
_S_EXTRA = 24


def _abmix_kernel(z_ref, zp_ref, zn_ref, x_ref, gate_ref, alng_ref, alnb_ref, ws_ref, bs_ref,
                  cw_ref, cb_ref, blng_ref, blnb_ref, wout_ref, o_ref,
                  e_ref, s_ref, hc_ref, y_ref, *, tm, tiles_per_seg):
    i = pl.program_id(0)
    pos = i % tiles_per_seg
    nch = tm // CHUNK

    ga = jax.nn.gelu(z_ref[:, :2 * A_WIDTH], approximate=True)
    u = ga[:, :A_WIDTH]
    vb = _layernorm(ga[:, A_WIDTH:], alng_ref[...], alnb_ref[...]).astype(BF16)
    for h in range(A_HEADS):
        cols = slice(h * CHUNK, (h + 1) * CHUNK)
        rhs = jnp.concatenate([vb[c * CHUNK:(c + 1) * CHUNK, cols] for c in range(nch)], axis=1)
        r = _dot(ws_ref[h], rhs)
        for c in range(nch):
            rows = slice(c * CHUNK, (c + 1) * CHUNK)
            mixed = r[:, c * CHUNK:(c + 1) * CHUNK] + bs_ref[:, cols]
            y_ref[rows, cols] = (u[rows, cols] * mixed).astype(BF16)

    def glu(zz):
        return zz[:, :B_WIDTH] * jax.nn.sigmoid(zz[:, B_WIDTH:])

    e_ref[0:HALO, :] = jnp.where(pos == 0, 0.0, glu(zp_ref[...]))
    e_ref[HALO:HALO + tm, :] = glu(z_ref[:, 2 * A_WIDTH:])
    e_ref[HALO + tm:, :] = jnp.where(pos == tiles_per_seg - 1, 0.0, glu(zn_ref[...]))

    base = HALO - B_PAD
    for cb in range(B_WIDTH // _CONV_LANES):
        lanes = slice(cb * _CONV_LANES, (cb + 1) * _CONV_LANES)
        for r in range(8):
            s_ref[r] = e_ref[pl.ds(r, tm + _S_EXTRA), lanes]

        def row_block(rb, carry, lanes=lanes):
            r0 = pl.multiple_of(rb * _CONV_ROWS, _CONV_ROWS)
            acc = jnp.zeros((_CONV_ROWS, _CONV_LANES), F32)
            for k in range(B_CONV):
                off = base + k
                tap = s_ref[off % 8, pl.ds(r0 + 8 * (off // 8), _CONV_ROWS), :]
                acc = acc + tap * cw_ref[k:k + 1, lanes]
            hc_ref[pl.ds(r0, _CONV_ROWS), lanes] = acc + cb_ref[:, lanes]
            return carry

        lax.fori_loop(0, tm // _CONV_ROWS, row_block, 0)

    hb = _layernorm(hc_ref[...], blng_ref[...], blnb_ref[...])
    y_ref[:, A_WIDTH:] = (hb * jax.nn.sigmoid(hb)).astype(BF16)

    o_ref[...] = x_ref[...] + gate_ref[...] * _dot(y_ref[...], wout_ref[...])


def _abmix(z, x, gate, p, *, tm, seg_len, tiles_per_batch):
    m, d = x.shape
    tiles_per_seg = seg_len // tm
    hb = tm // HALO
    nhalo = m // HALO
    kern = functools.partial(_abmix_kernel, tm=tm, tiles_per_seg=tiles_per_seg)
    row = lambda w: pl.BlockSpec((1, w), lambda i: (0, 0))
    return pl.pallas_call(
        kern,
        grid=(m // tm,),
        in_specs=[
            pl.BlockSpec((tm, 2 * A_WIDTH + 2 * B_WIDTH), lambda i: (i, 0)),
            pl.BlockSpec((HALO, 2 * B_WIDTH), lambda i: (jnp.maximum(i * hb - 1, 0), 1)),
            pl.BlockSpec((HALO, 2 * B_WIDTH), lambda i: (jnp.minimum((i + 1) * hb, nhalo - 1), 1)),
            pl.BlockSpec((tm, d), lambda i: (i, 0)),
            pl.BlockSpec((None, 1, d), lambda i: (i // tiles_per_batch, 0, 0)),
            row(A_WIDTH), row(A_WIDTH),
            _const_spec((A_HEADS, CHUNK, CHUNK)),
            _const_spec((CHUNK, A_WIDTH)),
            _const_spec((B_CONV, B_WIDTH)),
            row(B_WIDTH), row(B_WIDTH), row(B_WIDTH),
            _const_spec((A_WIDTH + B_WIDTH, d)),
        ],
        out_specs=pl.BlockSpec((tm, d), lambda i: (i, 0)),
        out_shape=jax.ShapeDtypeStruct((m, d), F32),
        scratch_shapes=[
            pltpu.VMEM((tm + 2 * HALO, B_WIDTH), F32),
            pltpu.VMEM((8, tm + _S_EXTRA, _CONV_LANES), F32),
            pltpu.VMEM((tm, B_WIDTH), F32),
            pltpu.VMEM((tm, A_WIDTH + B_WIDTH), BF16),
        ],
        compiler_params=_cparams("parallel"),
        name="ab_mixer",
    )(z, z, z, x, gate, p["a_ln_g"], p["a_ln_b"], p["w_s"], p["b_s_full"], p["conv_w"], p["conv_b"],
      p["b_ln_g"], p["b_ln_b"], p["w_out"])


def _ffn_up_kernel(x_ref, xp_ref, xn_ref, ng_ref, sh_ref, sc_ref, wg_ref, wu_ref, cw_ref, cb_ref,
                   a_ref, h_ref, g_ref, *, tm, tiles_per_seg):
    i = pl.program_id(0)
    f = pl.program_id(1)
    pos = i % tiles_per_seg

    @pl.when(f == 0)
    def _():
        def hmod(xx):
            return _norm_mod(xx, ng_ref[...], sh_ref[...], sc_ref[...])

        h_ref[0:HALO, :] = jnp.where(pos == 0, 0.0, hmod(xp_ref[...])).astype(BF16)
        h_ref[HALO:HALO + tm, :] = hmod(x_ref[...]).astype(BF16)
        h_ref[HALO + tm:, :] = jnp.where(pos == tiles_per_seg - 1, 0.0, hmod(xn_ref[...])).astype(BF16)

    g_ref[...] = _dot(h_ref[...], wg_ref[...])
    u = _dot(h_ref[HALO:HALO + tm, :], wu_ref[...])
    gc = (cw_ref[0:1, :] * g_ref[pl.ds(HALO - 1, tm), :] + cw_ref[1:2, :] * g_ref[pl.ds(HALO, tm), :]
          + cw_ref[2:3, :] * g_ref[pl.ds(HALO + 1, tm), :] + cb_ref[...])
    a_ref[...] = (gc * jax.nn.sigmoid(gc) * u).astype(BF16)


def _ffn_down_kernel(a_ref, x_ref, gate_ref, wd_ref, fg_ref, o_ref, *, final_norm):
    y = x_ref[...] + gate_ref[...] * _dot(a_ref[...], wd_ref[...])
    if final_norm:
        y = _rms(y, fg_ref[...])
    o_ref[...] = y


def _ffn(x, ng, shift, scale, gate, w_up, conv_w, conv_b, w_down, fg, *, tm_up, tf, tm_down, seg_len,
         final_norm):
    m, d = x.shape
    nf = D_FF // tf
    hb = tm_up // HALO
    nhalo = m // HALO
    up_tiles_per_batch = m // shift.shape[0] // tm_up
    kern = functools.partial(_ffn_up_kernel, tm=tm_up, tiles_per_seg=seg_len // tm_up)
    vec = pl.BlockSpec((None, 1, d), lambda i, f: (i // up_tiles_per_batch, 0, 0))
    row = pl.BlockSpec((1, d), lambda i, f: (0, 0))
    a = pl.pallas_call(
        kern,
        grid=(m // tm_up, nf),
        in_specs=[
            pl.BlockSpec((tm_up, d), lambda i, f: (i, 0)),
            pl.BlockSpec((HALO, d), lambda i, f: (jnp.maximum(i * hb - 1, 0), 0)),
            pl.BlockSpec((HALO, d), lambda i, f: (jnp.minimum((i + 1) * hb, nhalo - 1), 0)),
            row, vec, vec,
            pl.BlockSpec((d, tf), lambda i, f: (0, f)),
            pl.BlockSpec((d, tf), lambda i, f: (0, nf + f)),
            pl.BlockSpec((3, tf), lambda i, f: (0, f)),
            pl.BlockSpec((1, tf), lambda i, f: (0, f)),
        ],
        out_specs=pl.BlockSpec((tm_up, tf), lambda i, f: (i, f)),
        out_shape=jax.ShapeDtypeStruct((m, D_FF), BF16),
        scratch_shapes=[
            pltpu.VMEM((tm_up + 2 * HALO, d), BF16),
            pltpu.VMEM((tm_up + 2 * HALO, tf), F32),
        ],
        compiler_params=_cparams("parallel", "arbitrary"),
        name="ffn_up_conv_gate",
    )(x, x, x, ng, shift, scale, w_up, w_up, conv_w, conv_b)

    down_tiles_per_batch = up_tiles_per_batch * (tm_up // tm_down)
    return pl.pallas_call(
        functools.partial(_ffn_down_kernel, final_norm=final_norm),
        grid=(m // tm_down,),
        in_specs=[
            pl.BlockSpec((tm_down, D_FF), lambda i: (i, 0)),
            pl.BlockSpec((tm_down, d), lambda i: (i, 0)),
            pl.BlockSpec((None, 1, d), lambda i: (i // down_tiles_per_batch, 0, 0)),
            _const_spec((D_FF, d)),
            pl.BlockSpec((1, d), lambda i: (0, 0)),
        ],
        out_specs=pl.BlockSpec((tm_down, d), lambda i: (i, 0)),
        out_shape=jax.ShapeDtypeStruct((m, d), F32),
        compiler_params=_cparams("parallel"),
        name="ffn_down",
    )(a, x, gate, w_down, fg)


_MLA_TM = 256
_ZPAD = Q_LORA + KV_LORA + LANES


def _rope_tile(xt, c, s1, s2, odd):
    if odd:
        return pltpu.roll(xt, 64, 1) * c + pltpu.roll(xt, 32, 1) * s1 + pltpu.roll(xt, 96, 1) * s2
    return xt * c + pltpu.roll(xt, 96, 1) * s1 + pltpu.roll(xt, 32, 1) * s2


def _mla_proj_kernel(xl_ref, xc_ref, ng_ref, sh_ref, sc_ref, win_ref, qg_ref, kvg_ref, wqn_ref, wqp_ref,
                     wuk_ref, wuv_ref, c_ref, s1_ref, s2_ref, q_ref, k_ref, v_ref, *, n_lat_tiles, q_scale):
    t = pl.program_id(1)
    is_ctx = t >= n_lat_tiles
    x = jnp.where(is_ctx, xc_ref[...], xl_ref[...])
    h = _norm_mod(x, ng_ref[...], sh_ref[...], sc_ref[...]).astype(BF16)
    z = _dot(h, win_ref[...])
    c, s1, s2 = c_ref[...], s1_ref[...], s2_ref[...]

    ckvn = _rms(z[:, Q_LORA:Q_LORA + KV_LORA], kvg_ref[...]).astype(BF16)
    kpe = _rope_tile(z[:, Q_LORA + KV_LORA:], c, s1, s2, False).astype(BF16)
    kn = _dot(ckvn, wuk_ref[...])
    v_ref[...] = _dot(ckvn, wuv_ref[...]).astype(BF16)
    for hd in range(MLA_HEADS):
        k_ref[:, hd * HEAD_PAD:hd * HEAD_PAD + QK_NOPE] = kn[:, hd * QK_NOPE:(hd + 1) * QK_NOPE].astype(BF16)
        k_ref[:, hd * HEAD_PAD + QK_NOPE:(hd + 1) * HEAD_PAD] = kpe

    @pl.when(jnp.logical_not(is_ctx))
    def _():
        cqn = _rms(z[:, :Q_LORA], qg_ref[...]).astype(BF16)
        qn = _dot(cqn, wqn_ref[...])
        qp = _dot(cqn, wqp_ref[...])
        for hd in range(MLA_HEADS):
            q_ref[:, hd * HEAD_PAD:hd * HEAD_PAD + QK_NOPE] = (
                qn[:, hd * QK_NOPE:(hd + 1) * QK_NOPE] * q_scale).astype(BF16)
            xt = qp[:, (hd // 2) * LANES:(hd // 2 + 1) * LANES]
            q_ref[:, hd * HEAD_PAD + QK_NOPE:(hd + 1) * HEAD_PAD] = (
                _rope_tile(xt, c, s1, s2, hd % 2 == 1) * q_scale).astype(BF16)


def _mla_proj(xl, xc, ng, shift, scale, p, tabs):
    bsz, seq, d = xl.shape
    lc = xc.shape[1]
    tm = _MLA_TM
    assert lc == tm
    nlt = seq // tm
    nt = nlt + 1
    kern = functools.partial(_mla_proj_kernel, n_lat_tiles=nlt,
                             q_scale=float((QK_NOPE + QK_ROPE) ** -0.5 * LOG2_E))
    vec = pl.BlockSpec((None, None, 1, d), lambda b, t: (b, t // nlt, 0, 0))
    tab = pl.BlockSpec((tm, LANES), lambda b, t: (t, 0))
    hw = MLA_HEADS * HEAD_PAD
    return pl.pallas_call(
        kern,
        grid=(bsz, nt),
        in_specs=[
            pl.BlockSpec((None, tm, d), lambda b, t: (b, jnp.minimum(t, nlt - 1), 0)),
            pl.BlockSpec((None, tm, d), lambda b, t: (b, 0, 0)),
            pl.BlockSpec((1, d), lambda b, t: (0, 0)),
            vec, vec,
            _const_spec((d, _ZPAD)),
            pl.BlockSpec((1, Q_LORA), lambda b, t: (0, 0)),
            pl.BlockSpec((1, KV_LORA), lambda b, t: (0, 0)),
            _const_spec((Q_LORA, MLA_HEADS * QK_NOPE)),
            _const_spec((Q_LORA, MLA_HEADS * QK_ROPE)),
            _const_spec((KV_LORA, MLA_HEADS * QK_NOPE)),
            _const_spec((KV_LORA, MLA_HEADS * V_DIM)),
            tab, tab, tab,
        ],
        out_specs=[
            pl.BlockSpec((None, tm, hw), lambda b, t: (b, jnp.minimum(t, nlt - 1), 0)),
            pl.BlockSpec((None, tm, hw), lambda b, t: (b, t, 0)),
            pl.BlockSpec((None, tm, MLA_HEADS * V_DIM), lambda b, t: (b, t, 0)),
        ],
        out_shape=[
            jax.ShapeDtypeStruct((bsz, seq, hw), BF16),
            jax.ShapeDtypeStruct((bsz, seq + lc, hw), BF16),
            jax.ShapeDtypeStruct((bsz, seq + lc, MLA_HEADS * V_DIM), BF16),
        ],
        compiler_params=_cparams("parallel", "arbitrary"),
        name="mla_proj",
    )(xl, xc, ng, shift, scale, p["w_in"], p["q_norm_g"], p["kv_norm_g"], p["w_uq_n"], p["w_uq_p"],
      p["w_uk"], p["w_uv"], *tabs)


def _key_chunks(lk, n):
    tiles = lk // HEAD_PAD
    assert tiles * HEAD_PAD == lk and tiles >= n
    sizes = [(tiles // n + (1 if c >= n - tiles % n else 0)) * HEAD_PAD for c in range(n)]
    starts = [sum(sizes[:c]) for c in range(n)]
    return list(zip(starts, sizes))


def _attn_kernel(q_ref, k_ref, v_ref, o_ref, *, chunks):
    q = q_ref[...]
    m = l = acc = None
    for start, size in chunks:
        s = lax.dot_general(q, k_ref[start:start + size, :], (((1,), (1,)), ((), ())),
                            preferred_element_type=F32)
        mc = jnp.max(s, axis=-1, keepdims=True)
        m_new = mc if m is None else jnp.maximum(m, mc)
        p = jnp.exp2(s - m_new)
        lc = jnp.sum(p, axis=-1, keepdims=True)
        pv = _dot(p.astype(BF16), v_ref[start:start + size, :])
        if m is None:
            l, acc = lc, pv
        else:
            alpha = jnp.exp2(m - m_new)
            l = alpha * l + lc
            acc = alpha * acc + pv
        m = m_new
    o_ref[...] = (acc / l).astype(o_ref.dtype)


def _attention(q, k, v, *, tq, n_chunks):
    bsz, lq, _ = q.shape
    lk = k.shape[1]
    return pl.pallas_call(
        functools.partial(_attn_kernel, chunks=_key_chunks(lk, n_chunks)),
        grid=(bsz, MLA_HEADS, lq // tq),
        in_specs=[
            pl.BlockSpec((None, tq, HEAD_PAD), lambda b, h, i: (b, i, h)),
            pl.BlockSpec((None, lk, HEAD_PAD), lambda b, h, i: (b, 0, h)),
            pl.BlockSpec((None, lk, V_DIM), lambda b, h, i: (b, 0, h)),
        ],
        out_specs=pl.BlockSpec((None, tq, V_DIM), lambda b, h, i: (b, i, h)),
        out_shape=jax.ShapeDtypeStruct((bsz, lq, MLA_HEADS * V_DIM), BF16),
        compiler_params=_cparams("parallel", "parallel", "arbitrary"),
        name="mla_attention",
    )(q, k, v)


def _oproj_kernel(a_ref, x_ref, gate_ref, w_ref, o_ref):
    o_ref[...] = x_ref[...] + gate_ref[...] * _dot(a_ref[...], w_ref[...])


def _oproj(a, x, gate, w, *, tm, tiles_per_batch):
    m, d = x.shape
    kdim = a.shape[1]
    return pl.pallas_call(
        _oproj_kernel,
        grid=(m // tm,),
        in_specs=[
            pl.BlockSpec((tm, kdim), lambda i: (i, 0)),
            pl.BlockSpec((tm, d), lambda i: (i, 0)),
            pl.BlockSpec((None, 1, d), lambda i: (i // tiles_per_batch, 0, 0)),
            _const_spec((kdim, d)),
        ],
        out_specs=pl.BlockSpec((tm, d), lambda i: (i, 0)),
        out_shape=jax.ShapeDtypeStruct((m, d), F32),
        compiler_params=_cparams("parallel"),
        name="attn_out_proj",
    )(a, x, gate, w)


def _rope_tables(seq, ctx_len):
    rows = seq // GRID_W
    row = jnp.repeat(jnp.arange(rows, dtype=F32), GRID_W)
    col = jnp.tile(jnp.arange(GRID_W, dtype=F32), rows)
    n_freq = QK_ROPE // 4
    inv = ROPE_THETA ** (-jnp.arange(n_freq, dtype=F32) / n_freq)
    ang = jnp.concatenate([row[:, None] * inv, col[:, None] * inv], axis=-1)
    cos, sin = jnp.cos(ang), jnp.sin(ang)
    half = QK_ROPE // 2
    cos = jnp.concatenate([cos, jnp.ones((ctx_len, half), F32)], axis=0)
    sin = jnp.concatenate([sin, jnp.zeros((ctx_len, half), F32)], axis=0)
    zh = jnp.zeros_like(cos)
    c = jnp.concatenate([cos, cos, zh, zh], axis=-1)
    s1 = jnp.concatenate([-sin, zh, zh, zh], axis=-1)
    s2 = jnp.concatenate([zh, sin, zh, zh], axis=-1)
    return c, s1, s2


def kernel(x, c, ctx, c_ctx, norm1_g, norm2_g, w_ada, b_ada, ab_w_in, ab_b_in, a_ln_g, a_ln_b, a_w_s, a_b_s, b_conv_w, b_conv_b, b_ln_g, b_ln_b, ab_w_out, mla_w_in, mla_q_norm_g, mla_w_uq, mla_kv_norm_g, mla_w_ukv, mla_w_o, ffn_w_up, ffn_conv_w, ffn_conv_b, ffn_w_down, final_norm_g):
    bsz, seq, d = x.shape
    lc = ctx.shape[1]
    depth = w_ada.shape[0]

    cv = jnp.concatenate([c, c_ctx[None, :], jnp.zeros((8 - bsz - 1, d), F32)], axis=0)
    mods = _adaln(cv, w_ada, b_ada)

    def lat_mod(i, k):
        return mods[i, :bsz, k * d:(k + 1) * d].reshape(bsz, 1, d)

    def ctx_mod(i, k):
        return mods[i, bsz:bsz + 1, k * d:(k + 1) * d].reshape(1, 1, d)

    xl = x.reshape(bsz * seq, d)
    xc = ctx.reshape(bsz * lc, d)
    n_ctx_rows = bsz * lc

    for i in range(depth):
        last = i == depth - 1
        even = i % 2 == 0
        j = i // 2
        ctx_update = not last
        ng1 = norm1_g[i].reshape(1, d)
        ng2 = norm2_g[i].reshape(1, d)
        if even:
            p = {
                "a_ln_g": a_ln_g[j].reshape(1, -1), "a_ln_b": a_ln_b[j].reshape(1, -1),
                "w_s": a_w_s[j].astype(BF16),
                "b_s_full": jnp.repeat(a_b_s[j].T, A_WIDTH // A_HEADS, axis=1),
                "conv_w": b_conv_w[j], "conv_b": b_conv_b[j].reshape(1, -1),
                "b_ln_g": b_ln_g[j].reshape(1, -1), "b_ln_b": b_ln_b[j].reshape(1, -1),
                "w_out": ab_w_out[j].astype(BF16),
            }
            w_in = ab_w_in[j].astype(BF16)
            b_in = ab_b_in[j].reshape(1, -1)
            z = _nmm(xl, ng1, lat_mod(i, 0), lat_mod(i, 1), w_in, b_in, tm=1024, tn=1024,
                     tiles_per_batch=seq // 1024)
            xl = _abmix(z, xl, lat_mod(i, 2), p, tm=256, seg_len=seq, tiles_per_batch=seq // 256)
            if ctx_update:
                zc = _nmm(xc, ng1, ctx_mod(i, 0), ctx_mod(i, 1), w_in, b_in, tm=n_ctx_rows, tn=1024,
                          tiles_per_batch=1)
                xc = _abmix(zc, xc, ctx_mod(i, 2), p, tm=256, seg_len=lc, tiles_per_batch=n_ctx_rows // 256)
        else:
            w_uq = mla_w_uq[j].reshape(Q_LORA, MLA_HEADS, QK_NOPE + QK_ROPE)
            w_ukv = mla_w_ukv[j].reshape(KV_LORA, MLA_HEADS, QK_NOPE + V_DIM)
            p = {
                "w_in": jnp.pad(mla_w_in[j], ((0, 0), (0, LANES - QK_ROPE))).astype(BF16),
                "q_norm_g": mla_q_norm_g[j].reshape(1, -1), "kv_norm_g": mla_kv_norm_g[j].reshape(1, -1),
                "w_uq_n": w_uq[:, :, :QK_NOPE].reshape(Q_LORA, -1).astype(BF16),
                "w_uq_p": w_uq[:, :, QK_NOPE:].reshape(Q_LORA, -1).astype(BF16),
                "w_uk": w_ukv[:, :, :QK_NOPE].reshape(KV_LORA, -1).astype(BF16),
                "w_uv": w_ukv[:, :, QK_NOPE:].reshape(KV_LORA, -1).astype(BF16),
            }
            shift = jnp.stack([lat_mod(i, 0), jnp.broadcast_to(ctx_mod(i, 0), (bsz, 1, d))], axis=1)
            scale = jnp.stack([lat_mod(i, 1), jnp.broadcast_to(ctx_mod(i, 1), (bsz, 1, d))], axis=1)
            q, k, v = _mla_proj(xl.reshape(bsz, seq, d), xc.reshape(bsz, lc, d), ng1, shift, scale, p,
                                _rope_tables(seq, lc))
            a = _attention(q, k, v, tq=512, n_chunks=4)
            xl = _oproj(a.reshape(bsz * seq, -1), xl, lat_mod(i, 2), mla_w_o[j].astype(BF16), tm=512,
                        tiles_per_batch=seq // 512)
            assert not ctx_update, "context-side MLA output is only needed when a later layer reads it"
        w_up = ffn_w_up[i].astype(BF16)
        w_down = ffn_w_down[i].astype(BF16)
        cw, cb = ffn_conv_w[i], ffn_conv_b[i].reshape(1, -1)
        fg = final_norm_g.reshape(1, d)
        xl = _ffn(xl, ng2, lat_mod(i, 3), lat_mod(i, 4), lat_mod(i, 5), w_up, cw, cb, w_down, fg,
                  tm_up=1024, tf=512, tm_down=256, seg_len=seq, final_norm=last)
        if ctx_update:
            xc = _ffn(xc, ng2, ctx_mod(i, 3), ctx_mod(i, 4), ctx_mod(i, 5), w_up, cw, cb, w_down, fg,
                      tm_up=256, tf=512, tm_down=256, seg_len=lc, final_norm=False)
    return xl.reshape(bsz, seq, d)
```

```python
import functools

import jax
import jax.numpy as jnp
from jax import lax
from jax.experimental import pallas as pl
from jax.experimental.pallas import tpu as pltpu

F32 = jnp.float32
BF16 = jnp.bfloat16

D_MODEL = 2048
GRID_W = 64
EPS = 1e-6
N_MOD = 6
CHUNK = 128
A_HEADS = 8
A_WIDTH = 1024
B_WIDTH = 1024
B_CONV = 31
MLA_HEADS = 16
Q_LORA = 768
KV_LORA = 512
QK_NOPE = 128
QK_ROPE = 64
V_DIM = 128
ROPE_THETA = 10000.0
D_FF = 5632
LOG2_E = 1.4426950408889634

LANES = 128
BF16_SUBLANES = 16
HEAD_PAD = 256
VMEM_LIMIT = 56 * 2**20

HALO = BF16_SUBLANES
B_PAD = (B_CONV - 1) // 2


def _cparams(*sem):
    return pltpu.CompilerParams(dimension_semantics=sem, vmem_limit_bytes=VMEM_LIMIT)


def _const_spec(shape):
    n = len(shape)
    return pl.BlockSpec(shape, lambda *_: (0,) * n, pipeline_mode=pl.Buffered(1))


def _rms(x, g):
    return x * lax.rsqrt(jnp.mean(x * x, axis=-1, keepdims=True) + EPS) * g


def _layernorm(x, g, b):
    mu = jnp.mean(x, axis=-1, keepdims=True)
    xc = x - mu
    var = jnp.mean(xc * xc, axis=-1, keepdims=True)
    return xc * lax.rsqrt(var + EPS) * g + b


def _norm_mod(x, g, shift, scale):
    return _rms(x, g) * (1.0 + scale) + shift


def _dot(a, b):
    return jnp.dot(a, b, preferred_element_type=F32)


def _adaln_kernel(cv_ref, w_ref, b_ref, o_ref):
    cv = cv_ref[...]
    a = (cv * jax.nn.sigmoid(cv)).astype(BF16)
    o_ref[...] = _dot(a, w_ref[...].astype(BF16)) + b_ref[...]


def _adaln(cv, w_ada, b_ada):
    depth, d, n = w_ada.shape
    tn = 1024
    return pl.pallas_call(
        _adaln_kernel,
        grid=(depth, n // tn),
        in_specs=[
            pl.BlockSpec((8, d), lambda l, j: (0, 0)),
            pl.BlockSpec((None, d, tn), lambda l, j: (l, 0, j)),
            pl.BlockSpec((None, 1, tn), lambda l, j: (l, 0, j)),
        ],
        out_specs=pl.BlockSpec((None, 8, tn), lambda l, j: (l, 0, j)),
        out_shape=jax.ShapeDtypeStruct((depth, 8, n), F32),
        compiler_params=_cparams("parallel", "arbitrary"),
        name="adaln",
    )(cv, w_ada, b_ada.reshape(depth, 1, n))


def _nmm_kernel(x_ref, g_ref, sh_ref, sc_ref, w_ref, b_ref, o_ref, h_ref):
    @pl.when(pl.program_id(1) == 0)
    def _():
        h_ref[...] = _norm_mod(x_ref[...], g_ref[...], sh_ref[...], sc_ref[...]).astype(BF16)

    o_ref[...] = _dot(h_ref[...], w_ref[...]) + b_ref[...]


def _nmm(x, g, shift, scale, w, b, *, tm, tn, tiles_per_batch):
    m, d = x.shape
    n = w.shape[1]
    vec = pl.BlockSpec((None, 1, d), lambda i, j: (i // tiles_per_batch, 0, 0))
    return pl.pallas_call(
        _nmm_kernel,
        grid=(m // tm, n // tn),
        in_specs=[
            pl.BlockSpec((tm, d), lambda i, j: (i, 0)),
            pl.BlockSpec((1, d), lambda i, j: (0, 0)),
            vec, vec,
            pl.BlockSpec((d, tn), lambda i, j: (0, j)),
            pl.BlockSpec((1, tn), lambda i, j: (0, j)),
        ],
        out_specs=pl.BlockSpec((tm, tn), lambda i, j: (i, j)),
        out_shape=jax.ShapeDtypeStruct((m, n), F32),
        scratch_shapes=[pltpu.VMEM((tm, d), BF16)],
        compiler_params=_cparams("parallel", "arbitrary"),
        name="norm_mod_matmul",
    )(x, g, shift, scale, w, b)


_CONV_LANES = 256
_CONV_ROWS = 32
_S_EXTRA = 24


def _abmix_kernel(z_ref, zp_ref, zn_ref, x_ref, gate_ref, alng_ref, alnb_ref, ws_ref, bs_ref,
                  cw_ref, cb_ref, blng_ref, blnb_ref, wout_ref, o_ref,
                  e_ref, s_ref, hc_ref, y_ref, *, tm, tiles_per_seg):
    i = pl.program_id(0)
    pos = i % tiles_per_seg
    nch = tm // CHUNK

    ga = jax.nn.gelu(z_ref[:, :2 * A_WIDTH], approximate=True)
    u = ga[:, :A_WIDTH]
    vb = _layernorm(ga[:, A_WIDTH:], alng_ref[...], alnb_ref[...]).astype(BF16)
    for h in range(A_HEADS):
        cols = slice(h * CHUNK, (h + 1) * CHUNK)
        rhs = jnp.concatenate([vb[c * CHUNK:(c + 1) * CHUNK, cols] for c in range(nch)], axis=1)
        r = _dot(ws_ref[h], rhs)
        for c in range(nch):
            rows = slice(c * CHUNK, (c + 1) * CHUNK)
            mixed = r[:, c * CHUNK:(c + 1) * CHUNK] + bs_ref[:, cols]
            y_ref[rows, cols] = (u[rows, cols] * mixed).astype(BF16)

    def glu(zz):
        return zz[:, :B_WIDTH] * jax.nn.sigmoid(zz[:, B_WIDTH:])

    e_ref[0:HALO, :] = jnp.where(pos == 0, 0.0, glu(zp_ref[...]))
    e_ref[HALO:HALO + tm, :] = glu(z_ref[:, 2 * A_WIDTH:])
    e_ref[HALO + tm:, :] = jnp.where(pos == tiles_per_seg - 1, 0.0, glu(zn_ref[...]))

    base = HALO - B_PAD
    for cb in range(B_WIDTH // _CONV_LANES):
        lanes = slice(cb * _CONV_LANES, (cb + 1) * _CONV_LANES)
        for r in range(8):
            s_ref[r] = e_ref[pl.ds(r, tm + _S_EXTRA), lanes]

        def row_block(rb, carry, lanes=lanes):
            r0 = pl.multiple_of(rb * _CONV_ROWS, _CONV_ROWS)
            acc = jnp.zeros((_CONV_ROWS, _CONV_LANES), F32)
            for k in range(B_CONV):
                off = base + k
                tap = s_ref[off % 8, pl.ds(r0 + 8 * (off // 8), _CONV_ROWS), :]
                acc = acc + tap * cw_ref[k:k + 1, lanes]
            hc_ref[pl.ds(r0, _CONV_ROWS), lanes] = acc + cb_ref[:, lanes]
            return carry

        lax.fori_loop(0, tm // _CONV_ROWS, row_block, 0)

    hb = _layernorm(hc_ref[...], blng_ref[...], blnb_ref[...])
    y_ref[:, A_WIDTH:] = (hb * jax.nn.sigmoid(hb)).astype(BF16)

    o_ref[...] = x_ref[...] + gate_ref[...] * _dot(y_ref[...], wout_ref[...])


def _abmix(z, x, gate, p, *, tm, seg_len, tiles_per_batch):
    m, d = x.shape
    tiles_per_seg = seg_len // tm
    hb = tm // HALO
    nhalo = m // HALO
    kern = functools.partial(_abmix_kernel, tm=tm, tiles_per_seg=tiles_per_seg)
    row = lambda w: pl.BlockSpec((1, w), lambda i: (0, 0))
    return pl.pallas_call(
        kern,
        grid=(m // tm,),
        in_specs=[
            pl.BlockSpec((tm, 2 * A_WIDTH + 2 * B_WIDTH), lambda i: (i, 0)),
            pl.BlockSpec((HALO, 2 * B_WIDTH), lambda i: (jnp.maximum(i * hb - 1, 0), 1)),
            pl.BlockSpec((HALO, 2 * B_WIDTH), lambda i: (jnp.minimum((i + 1) * hb, nhalo - 1), 1)),
            pl.BlockSpec((tm, d), lambda i: (i, 0)),
            pl.BlockSpec((None, 1, d), lambda i: (i // tiles_per_batch, 0, 0)),
            row(A_WIDTH), row(A_WIDTH),
            _const_spec((A_HEADS, CHUNK, CHUNK)),
            _const_spec((CHUNK, A_WIDTH)),
            _const_spec((B_CONV, B_WIDTH)),
            row(B_WIDTH), row(B_WIDTH), row(B_WIDTH),
            _const_spec((A_WIDTH + B_WIDTH, d)),
        ],
        out_specs=pl.BlockSpec((tm, d), lambda i: (i, 0)),
        out_shape=jax.ShapeDtypeStruct((m, d), F32),
        scratch_shapes=[
            pltpu.VMEM((tm + 2 * HALO, B_WIDTH), F32),
            pltpu.VMEM((8, tm + _S_EXTRA, _CONV_LANES), F32),
            pltpu.VMEM((tm, B_WIDTH), F32),
            pltpu.VMEM((tm, A_WIDTH + B_WIDTH), BF16),
        ],
        compiler_params=_cparams("parallel"),
        name="ab_mixer",
    )(z, z, z, x, gate, p["a_ln_g"], p["a_ln_b"], p["w_s"], p["b_s_full"], p["conv_w"], p["conv_b"],
      p["b_ln_g"], p["b_ln_b"], p["w_out"])


def _ffn_up_kernel(x_ref, xp_ref, xn_ref, ng_ref, sh_ref, sc_ref, wg_ref, wu_ref, cw_ref, cb_ref,
                   a_ref, h_ref, g_ref, *, tm, tiles_per_seg):
    i = pl.program_id(0)
    f = pl.program_id(1)
    pos = i % tiles_per_seg

    @pl.when(f == 0)
    def _():
        def hmod(xx):
            return _norm_mod(xx, ng_ref[...], sh_ref[...], sc_ref[...])

        h_ref[0:HALO, :] = jnp.where(pos == 0, 0.0, hmod(xp_ref[...])).astype(BF16)
        h_ref[HALO:HALO + tm, :] = hmod(x_ref[...]).astype(BF16)
        h_ref[HALO + tm:, :] = jnp.where(pos == tiles_per_seg - 1, 0.0, hmod(xn_ref[...])).astype(BF16)

    g_ref[...] = _dot(h_ref[...], wg_ref[...].astype(BF16))
    u = _dot(h_ref[HALO:HALO + tm, :], wu_ref[...].astype(BF16))
    gc = (cw_ref[0:1, :] * g_ref[pl.ds(HALO - 1, tm), :] + cw_ref[1:2, :] * g_ref[pl.ds(HALO, tm), :]
          + cw_ref[2:3, :] * g_ref[pl.ds(HALO + 1, tm), :] + cb_ref[...])
    a_ref[...] = (gc * jax.nn.sigmoid(gc) * u).astype(BF16)


def _ffn_down_kernel(a_ref, x_ref, gate_ref, wd_ref, fg_ref, o_ref, *, final_norm):
    y = x_ref[...] + gate_ref[...] * _dot(a_ref[...], wd_ref[...])
    if final_norm:
        y = _rms(y, fg_ref[...])
    o_ref[...] = y


def _ffn(x, ng, shift, scale, gate, w_up, conv_w, conv_b, w_down, fg, *, layer, tm_up, tf, tm_down, seg_len,
         final_norm):
    m, d = x.shape
    nf = D_FF // tf
    hb = tm_up // HALO
    nhalo = m // HALO
    up_tiles_per_batch = m // shift.shape[0] // tm_up
    kern = functools.partial(_ffn_up_kernel, tm=tm_up, tiles_per_seg=seg_len // tm_up)
    vec = pl.BlockSpec((None, 1, d), lambda i, f: (i // up_tiles_per_batch, 0, 0))
    row = pl.BlockSpec((1, d), lambda i, f: (0, 0))
    a = pl.pallas_call(
        kern,
        grid=(m // tm_up, nf),
        in_specs=[
            pl.BlockSpec((tm_up, d), lambda i, f: (i, 0)),
            pl.BlockSpec((HALO, d), lambda i, f: (jnp.maximum(i * hb - 1, 0), 0)),
            pl.BlockSpec((HALO, d), lambda i, f: (jnp.minimum((i + 1) * hb, nhalo - 1), 0)),
            row, vec, vec,
            pl.BlockSpec((None, d, tf), lambda i, f: (layer, 0, f)),
            pl.BlockSpec((None, d, tf), lambda i, f: (layer, 0, nf + f)),
            pl.BlockSpec((None, 3, tf), lambda i, f: (layer, 0, f)),
            pl.BlockSpec((None, 1, tf), lambda i, f: (layer, 0, f)),
        ],
        out_specs=pl.BlockSpec((tm_up, tf), lambda i, f: (i, f)),
        out_shape=jax.ShapeDtypeStruct((m, D_FF), BF16),
        scratch_shapes=[
            pltpu.VMEM((tm_up + 2 * HALO, d), BF16),
            pltpu.VMEM((tm_up + 2 * HALO, tf), F32),
        ],
        compiler_params=_cparams("parallel", "arbitrary"),
        name="ffn_up_conv_gate",
    )(x, x, x, ng, shift, scale, w_up, w_up, conv_w, conv_b)

    down_tiles_per_batch = up_tiles_per_batch * (tm_up // tm_down)
    return pl.pallas_call(
        functools.partial(_ffn_down_kernel, final_norm=final_norm),
        grid=(m // tm_down,),
        in_specs=[
            pl.BlockSpec((tm_down, D_FF), lambda i: (i, 0)),
            pl.BlockSpec((tm_down, d), lambda i: (i, 0)),
            pl.BlockSpec((None, 1, d), lambda i: (i // down_tiles_per_batch, 0, 0)),
            pl.BlockSpec((None, D_FF, d), lambda i: (layer, 0, 0), pipeline_mode=pl.Buffered(1)),
            pl.BlockSpec((1, d), lambda i: (0, 0)),
        ],
        out_specs=pl.BlockSpec((tm_down, d), lambda i: (i, 0)),
        out_shape=jax.ShapeDtypeStruct((m, d), F32),
        compiler_params=_cparams("parallel"),
        name="ffn_down",
    )(a, x, gate, w_down, fg)


_MLA_TM = 256
_ZPAD = Q_LORA + KV_LORA + LANES


def _rope_tile(xt, c, s1, s2, odd):
    if odd:
        return pltpu.roll(xt, 64, 1) * c + pltpu.roll(xt, 32, 1) * s1 + pltpu.roll(xt, 96, 1) * s2
    return xt * c + pltpu.roll(xt, 96, 1) * s1 + pltpu.roll(xt, 32, 1) * s2


def _mla_proj_kernel(xl_ref, xc_ref, ng_ref, sh_ref, sc_ref, win_ref, qg_ref, kvg_ref, wqn_ref, wqp_ref,
                     wuk_ref, wuv_ref, c_ref, s1_ref, s2_ref, q_ref, k_ref, v_ref, *, n_lat_tiles, q_scale):
    t = pl.program_id(1)
    is_ctx = t >= n_lat_tiles
    x = jnp.where(is_ctx, xc_ref[...], xl_ref[...])
    h = _norm_mod(x, ng_ref[...], sh_ref[...], sc_ref[...]).astype(BF16)
    z = _dot(h, win_ref[...])
    c, s1, s2 = c_ref[...], s1_ref[...], s2_ref[...]

    ckvn = _rms(z[:, Q_LORA:Q_LORA + KV_LORA], kvg_ref[...]).astype(BF16)
    kpe = _rope_tile(z[:, Q_LORA + KV_LORA:], c, s1, s2, False).astype(BF16)
    kn = _dot(ckvn, wuk_ref[...])
    v_ref[...] = lax.dot_general(wuv_ref[...], ckvn, (((1,), (1,)), ((), ())),
                                 preferred_element_type=F32).astype(BF16)
    for hd in range(MLA_HEADS):
        k_ref[:, hd * HEAD_PAD:hd * HEAD_PAD + QK_NOPE] = kn[:, hd * QK_NOPE:(hd + 1) * QK_NOPE].astype(BF16)
        k_ref[:, hd * HEAD_PAD + QK_NOPE:(hd + 1) * HEAD_PAD] = kpe

    @pl.when(jnp.logical_not(is_ctx))
    def _():
        cqn = _rms(z[:, :Q_LORA], qg_ref[...]).astype(BF16)
        qn = _dot(cqn, wqn_ref[...])
        qp = _dot(cqn, wqp_ref[...])
        for hd in range(MLA_HEADS):
            q_ref[:, hd * HEAD_PAD:hd * HEAD_PAD + QK_NOPE] = (
                qn[:, hd * QK_NOPE:(hd + 1) * QK_NOPE] * q_scale).astype(BF16)
            xt = qp[:, (hd // 2) * LANES:(hd // 2 + 1) * LANES]
            q_ref[:, hd * HEAD_PAD + QK_NOPE:(hd + 1) * HEAD_PAD] = (
                _rope_tile(xt, c, s1, s2, hd % 2 == 1) * q_scale).astype(BF16)


def _mla_proj(xl, xc, ng, shift, scale, p, tabs):
    bsz, seq, d = xl.shape
    lc = xc.shape[1]
    tm = _MLA_TM
    assert lc == tm
    nlt = seq // tm
    nt = nlt + 1
    kern = functools.partial(_mla_proj_kernel, n_lat_tiles=nlt,
                             q_scale=float((QK_NOPE + QK_ROPE) ** -0.5 * LOG2_E))
    vec = pl.BlockSpec((None, None, 1, d), lambda b, t: (b, t // nlt, 0, 0))
    tab = pl.BlockSpec((tm, LANES), lambda b, t: (t, 0))
    hw = MLA_HEADS * HEAD_PAD
    return pl.pallas_call(
        kern,
        grid=(bsz, nt),
        in_specs=[
            pl.BlockSpec((None, tm, d), lambda b, t: (b, jnp.minimum(t, nlt - 1), 0)),
            pl.BlockSpec((None, tm, d), lambda b, t: (b, 0, 0)),
            pl.BlockSpec((1, d), lambda b, t: (0, 0)),
            vec, vec,
            _const_spec((d, _ZPAD)),
            pl.BlockSpec((1, Q_LORA), lambda b, t: (0, 0)),
            pl.BlockSpec((1, KV_LORA), lambda b, t: (0, 0)),
            _const_spec((Q_LORA, MLA_HEADS * QK_NOPE)),
            _const_spec((Q_LORA, MLA_HEADS * QK_ROPE)),
            _const_spec((KV_LORA, MLA_HEADS * QK_NOPE)),
            _const_spec((MLA_HEADS * V_DIM, KV_LORA)),
            tab, tab, tab,
        ],
        out_specs=[
            pl.BlockSpec((None, tm, hw), lambda b, t: (b, jnp.minimum(t, nlt - 1), 0)),
            pl.BlockSpec((None, tm, hw), lambda b, t: (b, t, 0)),
            pl.BlockSpec((None, MLA_HEADS * V_DIM, tm), lambda b, t: (b, 0, t)),
        ],
        out_shape=[
            jax.ShapeDtypeStruct((bsz, seq, hw), BF16),
            jax.ShapeDtypeStruct((bsz, seq + lc, hw), BF16),
            jax.ShapeDtypeStruct((bsz, MLA_HEADS * V_DIM, seq + lc), BF16),
        ],
        compiler_params=_cparams("parallel", "arbitrary"),
        name="mla_proj",
    )(xl, xc, ng, shift, scale, p["w_in"], p["q_norm_g"], p["kv_norm_g"], p["w_uq_n"], p["w_uq_p"],
      p["w_uk"], p["w_uv"], *tabs)


def _key_chunks(lk, n):
    tiles = lk // HEAD_PAD
    assert tiles * HEAD_PAD == lk and tiles >= n
    sizes = [(tiles // n + (1 if c >= n - tiles % n else 0)) * HEAD_PAD for c in range(n)]
    starts = [sum(sizes[:c]) for c in range(n)]
    return list(zip(starts, sizes))


def _attn_kernel(q_ref, k_ref, vt_ref, o_ref, *, chunks):
    q = q_ref[...]

    def scores(c):
        start, size = chunks[c]
        return lax.dot_general(k_ref[start:start + size, :], q, (((1,), (1,)), ((), ())),
                               preferred_element_type=F32)

    m = l = acc = None
    s_next = scores(0)
    for c, (start, size) in enumerate(chunks):
        s = s_next
        if c + 1 < len(chunks):
            s_next = scores(c + 1)
        mc = jnp.max(s, axis=0, keepdims=True)
        m_new = mc if m is None else jnp.maximum(m, mc)
        p = jnp.exp2(s - m_new)
        lc = jnp.sum(p, axis=0, keepdims=True)
        pv = _dot(vt_ref[:, start:start + size], p.astype(BF16))
        if m is None:
            l, acc = lc, pv
        else:
            alpha = jnp.exp2(m - m_new)
            l = alpha * l + lc
            acc = alpha * acc + pv
        m = m_new
    o_ref[...] = (acc / l).T.astype(o_ref.dtype)


def _attention(q, k, v, *, tq, n_chunks):
    bsz, lq, _ = q.shape
    lk = k.shape[1]
    return pl.pallas_call(
        functools.partial(_attn_kernel, chunks=_key_chunks(lk, n_chunks)),
        grid=(bsz, MLA_HEADS, lq // tq),
        in_specs=[
            pl.BlockSpec((None, tq, HEAD_PAD), lambda b, h, i: (b, i, h)),
            pl.BlockSpec((None, lk, HEAD_PAD), lambda b, h, i: (b, 0, h)),
            pl.BlockSpec((None, V_DIM, lk), lambda b, h, i: (b, h, 0)),
        ],
        out_specs=pl.BlockSpec((None, tq, V_DIM), lambda b, h, i: (b, i, h)),
        out_shape=jax.ShapeDtypeStruct((bsz, lq, MLA_HEADS * V_DIM), BF16),
        compiler_params=_cparams("parallel", "parallel", "arbitrary"),
        name="mla_attention",
    )(q, k, v)


def _oproj_kernel(a_ref, x_ref, gate_ref, w_ref, o_ref):
    o_ref[...] = x_ref[...] + gate_ref[...] * _dot(a_ref[...], w_ref[...])


def _oproj(a, x, gate, w, *, tm, tiles_per_batch):
    m, d = x.shape
    kdim = a.shape[1]
    return pl.pallas_call(
        _oproj_kernel,
        grid=(m // tm,),
        in_specs=[
            pl.BlockSpec((tm, kdim), lambda i: (i, 0)),
            pl.BlockSpec((tm, d), lambda i: (i, 0)),
            pl.BlockSpec((None, 1, d), lambda i: (i // tiles_per_batch, 0, 0)),
            _const_spec((kdim, d)),
        ],
        out_specs=pl.BlockSpec((tm, d), lambda i: (i, 0)),
        out_shape=jax.ShapeDtypeStruct((m, d), F32),
        compiler_params=_cparams("parallel"),
        name="attn_out_proj",
    )(a, x, gate, w)


def _rope_tables(seq, ctx_len):
    rows = seq // GRID_W
    row = jnp.repeat(jnp.arange(rows, dtype=F32), GRID_W)
    col = jnp.tile(jnp.arange(GRID_W, dtype=F32), rows)
    n_freq = QK_ROPE // 4
    inv = ROPE_THETA ** (-jnp.arange(n_freq, dtype=F32) / n_freq)
    ang = jnp.concatenate([row[:, None] * inv, col[:, None] * inv], axis=-1)
    cos, sin = jnp.cos(ang), jnp.sin(ang)
    half = QK_ROPE // 2
    cos = jnp.concatenate([cos, jnp.ones((ctx_len, half), F32)], axis=0)
    sin = jnp.concatenate([sin, jnp.zeros((ctx_len, half), F32)], axis=0)
    zh = jnp.zeros_like(cos)
    c = jnp.concatenate([cos, cos, zh, zh], axis=-1)
    s1 = jnp.concatenate([-sin, zh, zh, zh], axis=-1)
    s2 = jnp.concatenate([zh, sin, zh, zh], axis=-1)
    return c, s1, s2


def kernel(x, c, ctx, c_ctx, norm1_g, norm2_g, w_ada, b_ada, ab_w_in, ab_b_in, a_ln_g, a_ln_b, a_w_s, a_b_s, b_conv_w, b_conv_b, b_ln_g, b_ln_b, ab_w_out, mla_w_in, mla_q_norm_g, mla_w_uq, mla_kv_norm_g, mla_w_ukv, mla_w_o, ffn_w_up, ffn_conv_w, ffn_conv_b, ffn_w_down, final_norm_g):
    bsz, seq, d = x.shape
    lc = ctx.shape[1]
    depth = w_ada.shape[0]

    cv = jnp.concatenate([c, c_ctx[None, :], jnp.zeros((8 - bsz - 1, d), F32)], axis=0)
    mods = _adaln(cv, w_ada, b_ada)

    def lat_mod(i, k):
        return mods[i, :bsz, k * d:(k + 1) * d].reshape(bsz, 1, d)

    def ctx_mod(i, k):
        return mods[i, bsz:bsz + 1, k * d:(k + 1) * d].reshape(1, 1, d)

    ffn_w_down_bf16 = ffn_w_down.astype(BF16)
    ffn_conv_b_rows = ffn_conv_b.reshape(depth, 1, -1)

    xl = x.reshape(bsz * seq, d)
    xc = ctx.reshape(bsz * lc, d)
    n_ctx_rows = bsz * lc

    for i in range(depth):
        last = i == depth - 1
        even = i % 2 == 0
        j = i // 2
        ctx_update = not last
        ng1 = norm1_g[i].reshape(1, d)
        ng2 = norm2_g[i].reshape(1, d)
        if even:
            p = {
                "a_ln_g": a_ln_g[j].reshape(1, -1), "a_ln_b": a_ln_b[j].reshape(1, -1),
                "w_s": a_w_s[j].astype(BF16),
                "b_s_full": jnp.repeat(a_b_s[j].T, A_WIDTH // A_HEADS, axis=1),
                "conv_w": b_conv_w[j], "conv_b": b_conv_b[j].reshape(1, -1),
                "b_ln_g": b_ln_g[j].reshape(1, -1), "b_ln_b": b_ln_b[j].reshape(1, -1),
                "w_out": ab_w_out[j].astype(BF16),
            }
            w_in = ab_w_in[j].astype(BF16)
            b_in = ab_b_in[j].reshape(1, -1)
            z = _nmm(xl, ng1, lat_mod(i, 0), lat_mod(i, 1), w_in, b_in, tm=1024, tn=1024,
                     tiles_per_batch=seq // 1024)
            xl = _abmix(z, xl, lat_mod(i, 2), p, tm=256, seg_len=seq, tiles_per_batch=seq // 256)
            if ctx_update:
                zc = _nmm(xc, ng1, ctx_mod(i, 0), ctx_mod(i, 1), w_in, b_in, tm=n_ctx_rows, tn=1024,
                          tiles_per_batch=1)
                xc = _abmix(zc, xc, ctx_mod(i, 2), p, tm=256, seg_len=lc, tiles_per_batch=n_ctx_rows // 256)
        else:
            w_uq = mla_w_uq[j].reshape(Q_LORA, MLA_HEADS, QK_NOPE + QK_ROPE)
            w_ukv = mla_w_ukv[j].reshape(KV_LORA, MLA_HEADS, QK_NOPE + V_DIM)
            p = {
                "w_in": jnp.pad(mla_w_in[j], ((0, 0), (0, LANES - QK_ROPE))).astype(BF16),
                "q_norm_g": mla_q_norm_g[j].reshape(1, -1), "kv_norm_g": mla_kv_norm_g[j].reshape(1, -1),
                "w_uq_n": w_uq[:, :, :QK_NOPE].reshape(Q_LORA, -1).astype(BF16),
                "w_uq_p": w_uq[:, :, QK_NOPE:].reshape(Q_LORA, -1).astype(BF16),
                "w_uk": w_ukv[:, :, :QK_NOPE].reshape(KV_LORA, -1).astype(BF16),
                "w_uv": w_ukv[:, :, QK_NOPE:].reshape(KV_LORA, -1).T.astype(BF16),
            }
            shift = jnp.stack([lat_mod(i, 0), jnp.broadcast_to(ctx_mod(i, 0), (bsz, 1, d))], axis=1)
            scale = jnp.stack([lat_mod(i, 1), jnp.broadcast_to(ctx_mod(i, 1), (bsz, 1, d))], axis=1)
            q, k, v = _mla_proj(xl.reshape(bsz, seq, d), xc.reshape(bsz, lc, d), ng1, shift, scale, p,
                                _rope_tables(seq, lc))
            a = _attention(q, k, v, tq=1024, n_chunks=3)
            xl = _oproj(a.reshape(bsz * seq, -1), xl, lat_mod(i, 2), mla_w_o[j].astype(BF16), tm=512,
                        tiles_per_batch=seq // 512)
            assert not ctx_update, "context-side MLA output is only needed when a later layer reads it"
        ffn_w = (ffn_w_up, ffn_conv_w, ffn_conv_b_rows, ffn_w_down_bf16, final_norm_g.reshape(1, d))
        xl = _ffn(xl, ng2, lat_mod(i, 3), lat_mod(i, 4), lat_mod(i, 5), *ffn_w, layer=i,
                  tm_up=1024, tf=512, tm_down=256, seg_len=seq, final_norm=last)
        if ctx_update:
            xc = _ffn(xc, ng2, ctx_mod(i, 3), ctx_mod(i, 4), ctx_mod(i, 5), *ffn_w, layer=i,
                      tm_up=256, tf=512, tm_down=256, seg_len=lc, final_norm=False)
    return xl.reshape(bsz, seq, d)
```

```python
import functools

import jax
import jax.numpy as jnp
from jax import lax
from jax.experimental import pallas as pl
from jax.experimental.pallas import tpu as pltpu

F32 = jnp.float32
BF16 = jnp.bfloat16

D_MODEL = 2048
GRID_W = 64
EPS = 1e-6
N_MOD = 6
CHUNK = 128
A_HEADS = 8
A_WIDTH = 1024
B_WIDTH = 1024
B_CONV = 31
MLA_HEADS = 16
Q_LORA = 768
KV_LORA = 512
QK_NOPE = 128
QK_ROPE = 64
V_DIM = 128
ROPE_THETA = 10000.0
D_FF = 5632
LOG2_E = 1.4426950408889634

LANES = 128
BF16_SUBLANES = 16
HEAD_PAD = 256
VMEM_LIMIT = 56 * 2**20

HALO = BF16_SUBLANES
B_PAD = (B_CONV - 1) // 2


def _cparams(*sem):
    return pltpu.CompilerParams(dimension_semantics=sem, vmem_limit_bytes=VMEM_LIMIT)


def _const_spec(shape):
    n = len(shape)
    return pl.BlockSpec(shape, lambda *_: (0,) * n, pipeline_mode=pl.Buffered(1))


def _rms(x, g):
    return x * lax.rsqrt(jnp.mean(x * x, axis=-1, keepdims=True) + EPS) * g


def _layernorm(x, g, b):
    mu = jnp.mean(x, axis=-1, keepdims=True)
    xc = x - mu
    var = jnp.mean(xc * xc, axis=-1, keepdims=True)
    return xc * lax.rsqrt(var + EPS) * g + b


def _norm_mod(x, g, shift, scale):
    return _rms(x, g) * (1.0 + scale) + shift


def _dot(a, b):
    return jnp.dot(a, b, preferred_element_type=F32)


def _adaln_kernel(cv_ref, w_ref, b_ref, o_ref):
    cv = cv_ref[...]
    a = (cv * jax.nn.sigmoid(cv)).astype(BF16)
    o_ref[...] = _dot(a, w_ref[...].astype(BF16)) + b_ref[...]


def _adaln(cv, w_ada, b_ada):
    depth, d, n = w_ada.shape
    tn = 1024
    return pl.pallas_call(
        _adaln_kernel,
        grid=(depth, n // tn),
        in_specs=[
            pl.BlockSpec((8, d), lambda l, j: (0, 0)),
            pl.BlockSpec((None, d, tn), lambda l, j: (l, 0, j)),
            pl.BlockSpec((None, 1, tn), lambda l, j: (l, 0, j)),
        ],
        out_specs=pl.BlockSpec((None, 8, tn), lambda l, j: (l, 0, j)),
        out_shape=jax.ShapeDtypeStruct((depth, 8, n), F32),
        compiler_params=_cparams("parallel", "arbitrary"),
        name="adaln",
    )(cv, w_ada, b_ada.reshape(depth, 1, n))


def _nmm_kernel(x_ref, g_ref, sh_ref, sc_ref, w_ref, b_ref, o_ref, h_ref):
    @pl.when(pl.program_id(1) == 0)
    def _():
        h_ref[...] = _norm_mod(x_ref[...], g_ref[...], sh_ref[...], sc_ref[...]).astype(BF16)

    o_ref[...] = _dot(h_ref[...], w_ref[...]) + b_ref[...]


def _nmm(x, g, shift, scale, w, b, *, tm, tn, tiles_per_batch):
    m, d = x.shape
    n = w.shape[1]
    vec = pl.BlockSpec((None, 1, d), lambda i, j: (i // tiles_per_batch, 0, 0))
    return pl.pallas_call(
        _nmm_kernel,
        grid=(m // tm, n // tn),
        in_specs=[
            pl.BlockSpec((tm, d), lambda i, j: (i, 0)),
            pl.BlockSpec((1, d), lambda i, j: (0, 0)),
            vec, vec,
            pl.BlockSpec((d, tn), lambda i, j: (0, j)),
            pl.BlockSpec((1, tn), lambda i, j: (0, j)),
        ],
        out_specs=pl.BlockSpec((tm, tn), lambda i, j: (i, j)),
        out_shape=jax.ShapeDtypeStruct((m, n), F32),
        scratch_shapes=[pltpu.VMEM((tm, d), BF16)],
        compiler_params=_cparams("parallel", "arbitrary"),
        name="norm_mod_matmul",
    )(x, g, shift, scale, w, b)


_CONV_ROWS = 64


def _abmix_kernel(z_ref, zp_ref, zn_ref, x_ref, gate_ref, alng_ref, alnb_ref, ws_ref, bs_ref,
                  cw_ref, cb_ref, blng_ref, blnb_ref, wout_ref, o_ref,
                  e_ref, hc_ref, y_ref, *, tm, tiles_per_seg):
    i = pl.program_id(0)
    pos = i % tiles_per_seg
    nch = tm // CHUNK

    ga = jax.nn.gelu(z_ref[:, :2 * A_WIDTH], approximate=True)
    u = ga[:, :A_WIDTH]
    vb = _layernorm(ga[:, A_WIDTH:], alng_ref[...], alnb_ref[...]).astype(BF16)
    for h in range(A_HEADS):
        cols = slice(h * CHUNK, (h + 1) * CHUNK)
        rhs = jnp.concatenate([vb[c * CHUNK:(c + 1) * CHUNK, cols] for c in range(nch)], axis=1)
        r = _dot(ws_ref[h], rhs)
        for c in range(nch):
            rows = slice(c * CHUNK, (c + 1) * CHUNK)
            mixed = r[:, c * CHUNK:(c + 1) * CHUNK] + bs_ref[:, cols]
            y_ref[rows, cols] = (u[rows, cols] * mixed).astype(BF16)

    def glu(zz):
        return zz[:, :B_WIDTH] * jax.nn.sigmoid(zz[:, B_WIDTH:])

    def put(r0, val):
        for c in range(B_WIDTH // LANES):
            e_ref[c, r0:r0 + val.shape[0], :] = val[:, c * LANES:(c + 1) * LANES]

    put(0, jnp.where(pos == 0, 0.0, glu(zp_ref[...])))
    put(HALO, glu(z_ref[:, 2 * A_WIDTH:]))
    put(HALO + tm, jnp.where(pos == tiles_per_seg - 1, 0.0, glu(zn_ref[...])))

    base = HALO - B_PAD
    for c in range(B_WIDTH // LANES):
        lanes = slice(c * LANES, (c + 1) * LANES)
        wk = [cw_ref[k:k + 1, lanes] for k in range(B_CONV)]
        bias = cb_ref[:, lanes]

        def row_block(rb, carry, c=c, lanes=lanes, wk=wk, bias=bias):
            r0 = pl.multiple_of(rb * _CONV_ROWS, _CONV_ROWS)
            acc = e_ref[c, pl.ds(r0 + base, _CONV_ROWS), :] * wk[0]
            for k in range(1, B_CONV):
                acc = acc + e_ref[c, pl.ds(r0 + base + k, _CONV_ROWS), :] * wk[k]
            hc_ref[pl.ds(r0, _CONV_ROWS), lanes] = acc + bias
            return carry

        lax.fori_loop(0, tm // _CONV_ROWS, row_block, 0)

    hb = _layernorm(hc_ref[...], blng_ref[...], blnb_ref[...])
    y_ref[:, A_WIDTH:] = (hb * jax.nn.sigmoid(hb)).astype(BF16)

    o_ref[...] = x_ref[...] + gate_ref[...] * _dot(y_ref[...], wout_ref[...])


def _abmix(z, x, gate, p, *, tm, seg_len, tiles_per_batch):
    m, d = x.shape
    tiles_per_seg = seg_len // tm
    hb = tm // HALO
    nhalo = m // HALO
    kern = functools.partial(_abmix_kernel, tm=tm, tiles_per_seg=tiles_per_seg)
    row = lambda w: pl.BlockSpec((1, w), lambda i: (0, 0))
    return pl.pallas_call(
        kern,
        grid=(m // tm,),
        in_specs=[
            pl.BlockSpec((tm, 2 * A_WIDTH + 2 * B_WIDTH), lambda i: (i, 0)),
            pl.BlockSpec((HALO, 2 * B_WIDTH), lambda i: (jnp.maximum(i * hb - 1, 0), 1)),
            pl.BlockSpec((HALO, 2 * B_WIDTH), lambda i: (jnp.minimum((i + 1) * hb, nhalo - 1), 1)),
            pl.BlockSpec((tm, d), lambda i: (i, 0)),
            pl.BlockSpec((None, 1, d), lambda i: (i // tiles_per_batch, 0, 0)),
            row(A_WIDTH), row(A_WIDTH),
            _const_spec((A_HEADS, CHUNK, CHUNK)),
            _const_spec((CHUNK, A_WIDTH)),
            _const_spec((B_CONV, B_WIDTH)),
            row(B_WIDTH), row(B_WIDTH), row(B_WIDTH),
            _const_spec((A_WIDTH + B_WIDTH, d)),
        ],
        out_specs=pl.BlockSpec((tm, d), lambda i: (i, 0)),
        out_shape=jax.ShapeDtypeStruct((m, d), F32),
        scratch_shapes=[
            pltpu.VMEM((B_WIDTH // LANES, tm + 2 * HALO, LANES), F32),
            pltpu.VMEM((tm, B_WIDTH), F32),
            pltpu.VMEM((tm, A_WIDTH + B_WIDTH), BF16),
        ],
        compiler_params=_cparams("parallel"),
        name="ab_mixer",
    )(z, z, z, x, gate, p["a_ln_g"], p["a_ln_b"], p["w_s"], p["b_s_full"], p["conv_w"], p["conv_b"],
      p["b_ln_g"], p["b_ln_b"], p["w_out"])


def _ffn_up_kernel(x_ref, xp_ref, xn_ref, ng_ref, sh_ref, sc_ref, wg_ref, wu_ref, cw_ref, cb_ref,
                   a_ref, h_ref, g_ref, *, tm, seg_len):
    i = pl.program_id(0)
    f = pl.program_id(1)
    tiles_per_seg = max(seg_len // tm, 1)
    pos = i % tiles_per_seg

    @pl.when(f == 0)
    def _():
        def hmod(xx):
            return _norm_mod(xx, ng_ref[...], sh_ref[...], sc_ref[...])

        h_ref[0:HALO, :] = jnp.where(pos == 0, 0.0, hmod(xp_ref[...])).astype(BF16)
        h_ref[HALO:HALO + tm, :] = hmod(x_ref[...]).astype(BF16)
        h_ref[HALO + tm:, :] = jnp.where(pos == tiles_per_seg - 1, 0.0, hmod(xn_ref[...])).astype(BF16)

    g_ref[...] = _dot(h_ref[...], wg_ref[...].astype(BF16))
    u = _dot(h_ref[HALO:HALO + tm, :], wu_ref[...].astype(BF16))
    g_prev = g_ref[pl.ds(HALO - 1, tm), :]
    g_next = g_ref[pl.ds(HALO + 1, tm), :]
    if tm > seg_len:
        r = lax.broadcasted_iota(jnp.int32, g_prev.shape, 0) % seg_len
        g_prev = jnp.where(r == 0, 0.0, g_prev)
        g_next = jnp.where(r == seg_len - 1, 0.0, g_next)
    gc = (cw_ref[0:1, :] * g_prev + cw_ref[1:2, :] * g_ref[pl.ds(HALO, tm), :]
          + cw_ref[2:3, :] * g_next + cb_ref[...])
    a_ref[...] = (gc * jax.nn.sigmoid(gc) * u).astype(BF16)


def _ffn_down_kernel(a_ref, x_ref, gate_ref, wd_ref, fg_ref, o_ref, *, final_norm):
    y = x_ref[...] + gate_ref[...] * _dot(a_ref[...], wd_ref[...])
    if final_norm:
        y = _rms(y, fg_ref[...])
    o_ref[...] = y


def _ffn(x, ng, shift, scale, gate, w_up, conv_w, conv_b, w_down, fg, *, layer, tm_up, tf, tm_down, seg_len,
         final_norm):
    m, d = x.shape
    nf = D_FF // tf
    hb = tm_up // HALO
    nhalo = m // HALO
    up_tiles_per_batch = m // shift.shape[0] // tm_up
    assert seg_len % tm_up == 0 or tm_up % seg_len == 0
    kern = functools.partial(_ffn_up_kernel, tm=tm_up, seg_len=seg_len)
    vec = pl.BlockSpec((None, 1, d), lambda i, f: (i // up_tiles_per_batch, 0, 0))
    row = pl.BlockSpec((1, d), lambda i, f: (0, 0))
    a = pl.pallas_call(
        kern,
        grid=(m // tm_up, nf),
        in_specs=[
            pl.BlockSpec((tm_up, d), lambda i, f: (i, 0)),
            pl.BlockSpec((HALO, d), lambda i, f: (jnp.maximum(i * hb - 1, 0), 0)),
            pl.BlockSpec((HALO, d), lambda i, f: (jnp.minimum((i + 1) * hb, nhalo - 1), 0)),
            row, vec, vec,
            pl.BlockSpec((None, d, tf), lambda i, f: (layer, 0, f)),
            pl.BlockSpec((None, d, tf), lambda i, f: (layer, 0, nf + f)),
            pl.BlockSpec((None, 3, tf), lambda i, f: (layer, 0, f)),
            pl.BlockSpec((None, 1, tf), lambda i, f: (layer, 0, f)),
        ],
        out_specs=pl.BlockSpec((tm_up, tf), lambda i, f: (i, f)),
        out_shape=jax.ShapeDtypeStruct((m, D_FF), BF16),
        scratch_shapes=[
            pltpu.VMEM((tm_up + 2 * HALO, d), BF16),
            pltpu.VMEM((tm_up + 2 * HALO, tf), F32),
        ],
        compiler_params=_cparams("parallel", "arbitrary"),
        name="ffn_up_conv_gate",
    )(x, x, x, ng, shift, scale, w_up, w_up, conv_w, conv_b)

    down_tiles_per_batch = up_tiles_per_batch * (tm_up // tm_down)
    return pl.pallas_call(
        functools.partial(_ffn_down_kernel, final_norm=final_norm),
        grid=(m // tm_down,),
        in_specs=[
            pl.BlockSpec((tm_down, D_FF), lambda i: (i, 0)),
            pl.BlockSpec((tm_down, d), lambda i: (i, 0)),
            pl.BlockSpec((None, 1, d), lambda i: (i // down_tiles_per_batch, 0, 0)),
            pl.BlockSpec((None, D_FF, d), lambda i: (layer, 0, 0), pipeline_mode=pl.Buffered(1)),
            pl.BlockSpec((1, d), lambda i: (0, 0)),
        ],
        out_specs=pl.BlockSpec((tm_down, d), lambda i: (i, 0)),
        out_shape=jax.ShapeDtypeStruct((m, d), F32),
        compiler_params=_cparams("parallel"),
        name="ffn_down",
    )(a, x, gate, w_down, fg)


_MLA_TM = 256
_ZPAD = Q_LORA + KV_LORA + LANES


def _rope_tile(xt, c, s1, s2, odd):
    if odd:
        return pltpu.roll(xt, 64, 1) * c + pltpu.roll(xt, 32, 1) * s1 + pltpu.roll(xt, 96, 1) * s2
    return xt * c + pltpu.roll(xt, 96, 1) * s1 + pltpu.roll(xt, 32, 1) * s2


def _mla_proj_kernel(xl_ref, xc_ref, ng_ref, sh_ref, sc_ref, win_ref, qg_ref, kvg_ref, wqn_ref, wqp_ref,
                     wuk_ref, wuv_ref, c_ref, s1_ref, s2_ref, q_ref, k_ref, v_ref, *, n_lat_tiles, q_scale):
    t = pl.program_id(1)
    is_ctx = t >= n_lat_tiles
    x = jnp.where(is_ctx, xc_ref[...], xl_ref[...])
    h = _norm_mod(x, ng_ref[...], sh_ref[...], sc_ref[...]).astype(BF16)
    z = _dot(h, win_ref[...])
    c, s1, s2 = c_ref[...], s1_ref[...], s2_ref[...]

    ckvn = _rms(z[:, Q_LORA:Q_LORA + KV_LORA], kvg_ref[...]).astype(BF16)
    kpe = _rope_tile(z[:, Q_LORA + KV_LORA:], c, s1, s2, False).astype(BF16)
    kn = _dot(ckvn, wuk_ref[...])
    v_ref[...] = lax.dot_general(wuv_ref[...], ckvn, (((1,), (1,)), ((), ())),
                                 preferred_element_type=F32).astype(BF16)
    for hd in range(MLA_HEADS):
        k_ref[:, hd * HEAD_PAD:hd * HEAD_PAD + QK_NOPE] = kn[:, hd * QK_NOPE:(hd + 1) * QK_NOPE].astype(BF16)
        k_ref[:, hd * HEAD_PAD + QK_NOPE:(hd + 1) * HEAD_PAD] = kpe

    @pl.when(jnp.logical_not(is_ctx))
    def _():
        cqn = _rms(z[:, :Q_LORA], qg_ref[...]).astype(BF16)
        qn = _dot(cqn, wqn_ref[...])
        qp = _dot(cqn, wqp_ref[...])
        for hd in range(MLA_HEADS):
            q_ref[:, hd * HEAD_PAD:hd * HEAD_PAD + QK_NOPE] = (
                qn[:, hd * QK_NOPE:(hd + 1) * QK_NOPE] * q_scale).astype(BF16)
            xt = qp[:, (hd // 2) * LANES:(hd // 2 + 1) * LANES]
            q_ref[:, hd * HEAD_PAD + QK_NOPE:(hd + 1) * HEAD_PAD] = (
                _rope_tile(xt, c, s1, s2, hd % 2 == 1) * q_scale).astype(BF16)


def _mla_proj(xl, xc, ng, shift, scale, p, tabs):
    bsz, seq, d = xl.shape
    lc = xc.shape[1]
    tm = _MLA_TM
    assert lc == tm
    nlt = seq // tm
    nt = nlt + 1
    kern = functools.partial(_mla_proj_kernel, n_lat_tiles=nlt,
                             q_scale=float((QK_NOPE + QK_ROPE) ** -0.5 * LOG2_E))
    vec = pl.BlockSpec((None, None, 1, d), lambda b, t: (b, t // nlt, 0, 0))
    tab = pl.BlockSpec((tm, LANES), lambda b, t: (t, 0))
    hw = MLA_HEADS * HEAD_PAD
    return pl.pallas_call(
        kern,
        grid=(bsz, nt),
        in_specs=[
            pl.BlockSpec((None, tm, d), lambda b, t: (b, jnp.minimum(t, nlt - 1), 0)),
            pl.BlockSpec((None, tm, d), lambda b, t: (b, 0, 0)),
            pl.BlockSpec((1, d), lambda b, t: (0, 0)),
            vec, vec,
            _const_spec((d, _ZPAD)),
            pl.BlockSpec((1, Q_LORA), lambda b, t: (0, 0)),
            pl.BlockSpec((1, KV_LORA), lambda b, t: (0, 0)),
            _const_spec((Q_LORA, MLA_HEADS * QK_NOPE)),
            _const_spec((Q_LORA, MLA_HEADS * QK_ROPE)),
            _const_spec((KV_LORA, MLA_HEADS * QK_NOPE)),
            _const_spec((MLA_HEADS * V_DIM, KV_LORA)),
            tab, tab, tab,
        ],
        out_specs=[
            pl.BlockSpec((None, tm, hw), lambda b, t: (b, jnp.minimum(t, nlt - 1), 0)),
            pl.BlockSpec((None, tm, hw), lambda b, t: (b, t, 0)),
            pl.BlockSpec((None, MLA_HEADS * V_DIM, tm), lambda b, t: (b, 0, t)),
        ],
        out_shape=[
            jax.ShapeDtypeStruct((bsz, seq, hw), BF16),
            jax.ShapeDtypeStruct((bsz, seq + lc, hw), BF16),
            jax.ShapeDtypeStruct((bsz, MLA_HEADS * V_DIM, seq + lc), BF16),
        ],
        compiler_params=_cparams("parallel", "arbitrary"),
        name="mla_proj",
    )(xl, xc, ng, shift, scale, p["w_in"], p["q_norm_g"], p["kv_norm_g"], p["w_uq_n"], p["w_uq_p"],
      p["w_uk"], p["w_uv"], *tabs)


def _key_chunks(lk, n):
    tiles = lk // HEAD_PAD
    assert tiles * HEAD_PAD == lk and tiles >= n
    sizes = [(tiles // n + (1 if c >= n - tiles % n else 0)) * HEAD_PAD for c in range(n)]
    starts = [sum(sizes[:c]) for c in range(n)]
    return list(zip(starts, sizes))


def _attn_kernel(q_ref, k_ref, vt_ref, o_ref, *, chunks):
    q = q_ref[...]

    def scores(c):
        start, size = chunks[c]
        return lax.dot_general(k_ref[start:start + size, :], q, (((1,), (1,)), ((), ())),
                               preferred_element_type=F32)

    m = l = acc = None
    s_next = scores(0)
    for c, (start, size) in enumerate(chunks):
        s = s_next
        if c + 1 < len(chunks):
            s_next = scores(c + 1)
        mc = jnp.max(s, axis=0, keepdims=True)
        m_new = mc if m is None else jnp.maximum(m, mc)
        p = jnp.exp2(s - m_new)
        lc = jnp.sum(p, axis=0, keepdims=True)
        pv = _dot(vt_ref[:, start:start + size], p.astype(BF16))
        if m is None:
            l, acc = lc, pv
        else:
            alpha = jnp.exp2(m - m_new)
            l = alpha * l + lc
            acc = alpha * acc + pv
        m = m_new
    o_ref[...] = (acc / l).T.astype(o_ref.dtype)


def _attention(q, k, v, *, tq, n_chunks):
    bsz, lq, _ = q.shape
    lk = k.shape[1]
    return pl.pallas_call(
        functools.partial(_attn_kernel, chunks=_key_chunks(lk, n_chunks)),
        grid=(bsz, MLA_HEADS, lq // tq),
        in_specs=[
            pl.BlockSpec((None, tq, HEAD_PAD), lambda b, h, i: (b, i, h)),
            pl.BlockSpec((None, lk, HEAD_PAD), lambda b, h, i: (b, 0, h)),
            pl.BlockSpec((None, V_DIM, lk), lambda b, h, i: (b, h, 0)),
        ],
        out_specs=pl.BlockSpec((None, tq, V_DIM), lambda b, h, i: (b, i, h)),
        out_shape=jax.ShapeDtypeStruct((bsz, lq, MLA_HEADS * V_DIM), BF16),
        compiler_params=_cparams("parallel", "parallel", "arbitrary"),
        name="mla_attention",
    )(q, k, v)


def _oproj_kernel(a_ref, x_ref, gate_ref, w_ref, o_ref):
    o_ref[...] = x_ref[...] + gate_ref[...] * _dot(a_ref[...], w_ref[...])


def _oproj(a, x, gate, w, *, tm, tiles_per_batch):
    m, d = x.shape
    kdim = a.shape[1]
    return pl.pallas_call(
        _oproj_kernel,
        grid=(m // tm,),
        in_specs=[
            pl.BlockSpec((tm, kdim), lambda i: (i, 0)),
            pl.BlockSpec((tm, d), lambda i: (i, 0)),
            pl.BlockSpec((None, 1, d), lambda i: (i // tiles_per_batch, 0, 0)),
            _const_spec((kdim, d)),
        ],
        out_specs=pl.BlockSpec((tm, d), lambda i: (i, 0)),
        out_shape=jax.ShapeDtypeStruct((m, d), F32),
        compiler_params=_cparams("parallel"),
        name="attn_out_proj",
    )(a, x, gate, w)


def _rope_tables(seq, ctx_len):
    rows = seq // GRID_W
    row = jnp.repeat(jnp.arange(rows, dtype=F32), GRID_W)
    col = jnp.tile(jnp.arange(GRID_W, dtype=F32), rows)
    n_freq = QK_ROPE // 4
    inv = ROPE_THETA ** (-jnp.arange(n_freq, dtype=F32) / n_freq)
    ang = jnp.concatenate([row[:, None] * inv, col[:, None] * inv], axis=-1)
    cos, sin = jnp.cos(ang), jnp.sin(ang)
    half = QK_ROPE // 2
    cos = jnp.concatenate([cos, jnp.ones((ctx_len, half), F32)], axis=0)
    sin = jnp.concatenate([sin, jnp.zeros((ctx_len, half), F32)], axis=0)
    zh = jnp.zeros_like(cos)
    c = jnp.concatenate([cos, cos, zh, zh], axis=-1)
    s1 = jnp.concatenate([-sin, zh, zh, zh], axis=-1)
    s2 = jnp.concatenate([zh, sin, zh, zh], axis=-1)
    return c, s1, s2


def kernel(x, c, ctx, c_ctx, norm1_g, norm2_g, w_ada, b_ada, ab_w_in, ab_b_in, a_ln_g, a_ln_b, a_w_s, a_b_s, b_conv_w, b_conv_b, b_ln_g, b_ln_b, ab_w_out, mla_w_in, mla_q_norm_g, mla_w_uq, mla_kv_norm_g, mla_w_ukv, mla_w_o, ffn_w_up, ffn_conv_w, ffn_conv_b, ffn_w_down, final_norm_g):
    bsz, seq, d = x.shape
    lc = ctx.shape[1]
    depth = w_ada.shape[0]

    cv = jnp.concatenate([c, c_ctx[None, :], jnp.zeros((8 - bsz - 1, d), F32)], axis=0)
    mods = _adaln(cv, w_ada, b_ada)

    def lat_mod(i, k):
        return mods[i, :bsz, k * d:(k + 1) * d].reshape(bsz, 1, d)

    def ctx_mod(i, k):
        return mods[i, bsz:bsz + 1, k * d:(k + 1) * d].reshape(1, 1, d)

    ffn_w_down_bf16 = ffn_w_down.astype(BF16)
    ffn_conv_b_rows = ffn_conv_b.reshape(depth, 1, -1)

    xl = x.reshape(bsz * seq, d)
    xc = ctx.reshape(bsz * lc, d)
    n_ctx_rows = bsz * lc

    for i in range(depth):
        last = i == depth - 1
        even = i % 2 == 0
        j = i // 2
        ctx_update = not last
        ng1 = norm1_g[i].reshape(1, d)
        ng2 = norm2_g[i].reshape(1, d)
        if even:
            p = {
                "a_ln_g": a_ln_g[j].reshape(1, -1), "a_ln_b": a_ln_b[j].reshape(1, -1),
                "w_s": a_w_s[j].astype(BF16),
                "b_s_full": jnp.repeat(a_b_s[j].T, A_WIDTH // A_HEADS, axis=1),
                "conv_w": b_conv_w[j], "conv_b": b_conv_b[j].reshape(1, -1),
                "b_ln_g": b_ln_g[j].reshape(1, -1), "b_ln_b": b_ln_b[j].reshape(1, -1),
                "w_out": ab_w_out[j].astype(BF16),
            }
            w_in = ab_w_in[j].astype(BF16)
            b_in = ab_b_in[j].reshape(1, -1)
            z = _nmm(xl, ng1, lat_mod(i, 0), lat_mod(i, 1), w_in, b_in, tm=1024, tn=1024,
                     tiles_per_batch=seq // 1024)
            xl = _abmix(z, xl, lat_mod(i, 2), p, tm=256, seg_len=seq, tiles_per_batch=seq // 256)
            if ctx_update:
                zc = _nmm(xc, ng1, ctx_mod(i, 0), ctx_mod(i, 1), w_in, b_in, tm=n_ctx_rows, tn=1024,
                          tiles_per_batch=1)
                xc = _abmix(zc, xc, ctx_mod(i, 2), p, tm=256, seg_len=lc, tiles_per_batch=n_ctx_rows // 256)
        else:
            w_uq = mla_w_uq[j].reshape(Q_LORA, MLA_HEADS, QK_NOPE + QK_ROPE)
            w_ukv = mla_w_ukv[j].reshape(KV_LORA, MLA_HEADS, QK_NOPE + V_DIM)
            p = {
                "w_in": jnp.pad(mla_w_in[j], ((0, 0), (0, LANES - QK_ROPE))).astype(BF16),
                "q_norm_g": mla_q_norm_g[j].reshape(1, -1), "kv_norm_g": mla_kv_norm_g[j].reshape(1, -1),
                "w_uq_n": w_uq[:, :, :QK_NOPE].reshape(Q_LORA, -1).astype(BF16),
                "w_uq_p": w_uq[:, :, QK_NOPE:].reshape(Q_LORA, -1).astype(BF16),
                "w_uk": w_ukv[:, :, :QK_NOPE].reshape(KV_LORA, -1).astype(BF16),
                "w_uv": w_ukv[:, :, QK_NOPE:].reshape(KV_LORA, -1).T.astype(BF16),
            }
            shift = jnp.stack([lat_mod(i, 0), jnp.broadcast_to(ctx_mod(i, 0), (bsz, 1, d))], axis=1)
            scale = jnp.stack([lat_mod(i, 1), jnp.broadcast_to(ctx_mod(i, 1), (bsz, 1, d))], axis=1)
            q, k, v = _mla_proj(xl.reshape(bsz, seq, d), xc.reshape(bsz, lc, d), ng1, shift, scale, p,
                                _rope_tables(seq, lc))
            a = _attention(q, k, v, tq=1024, n_chunks=3)
            xl = _oproj(a.reshape(bsz * seq, -1), xl, lat_mod(i, 2), mla_w_o[j].astype(BF16), tm=512,
                        tiles_per_batch=seq // 512)
            assert not ctx_update, "context-side MLA output is only needed when a later layer reads it"
        ffn_w = (ffn_w_up, ffn_conv_w, ffn_conv_b_rows, ffn_w_down_bf16, final_norm_g.reshape(1, d))
        xl = _ffn(xl, ng2, lat_mod(i, 3), lat_mod(i, 4), lat_mod(i, 5), *ffn_w, layer=i,
                  tm_up=1024, tf=512, tm_down=256, seg_len=seq, final_norm=last)
        if ctx_update:
            xc = _ffn(xc, ng2, ctx_mod(i, 3), ctx_mod(i, 4), ctx_mod(i, 5), *ffn_w, layer=i,
                      tm_up=n_ctx_rows, tf=512, tm_down=256, seg_len=lc, final_norm=False)
    return xl.reshape(bsz, seq, d)
```

```python
import functools

import jax
import jax.numpy as jnp
from jax import lax
from jax.experimental import pallas as pl
from jax.experimental.pallas import tpu as pltpu

F32 = jnp.float32
BF16 = jnp.bfloat16

D_MODEL = 2048
GRID_W = 64
EPS = 1e-6
N_MOD = 6
CHUNK = 128
A_HEADS = 8
A_WIDTH = 1024
B_WIDTH = 1024
B_CONV = 31
MLA_HEADS = 16
Q_LORA = 768
KV_LORA = 512
QK_NOPE = 128
QK_ROPE = 64
V_DIM = 128
ROPE_THETA = 10000.0
D_FF = 5632
LOG2_E = 1.4426950408889634

LANES = 128
BF16_SUBLANES = 16
HEAD_PAD = 256
VMEM_LIMIT = 56 * 2**20

HALO = BF16_SUBLANES
B_PAD = (B_CONV - 1) // 2

NMM_TM, NMM_TN = 1024, 1024
ABMIX_TM = 256
FFN_UP_TM, FFN_TF = 1024, 512
FFN_DOWN_TM = 256
MLA_TM = 256
ATTN_TQ, ATTN_KEY_CHUNKS, ATTN_HEADS_PER_STEP = 1024, 3, 4
OPROJ_TM = 512
CTX_TM = 256


def _cparams(*sem):
    return pltpu.CompilerParams(dimension_semantics=sem, vmem_limit_bytes=VMEM_LIMIT)


def _const_spec(shape):
    n = len(shape)
    return pl.BlockSpec(shape, lambda *_: (0,) * n, pipeline_mode=pl.Buffered(1))


def _rms(x, g):
    return x * lax.rsqrt(jnp.mean(x * x, axis=-1, keepdims=True) + EPS) * g


def _layernorm(x, g, b):
    mu = jnp.mean(x, axis=-1, keepdims=True)
    xc = x - mu
    var = jnp.mean(xc * xc, axis=-1, keepdims=True)
    return xc * lax.rsqrt(var + EPS) * g + b


def _norm_mod(x, g, shift, scale):
    return _rms(x, g) * (1.0 + scale) + shift


def _dot(a, b):
    return jnp.dot(a, b, preferred_element_type=F32)


def _next_specs(d, tiles_per_batch):
    vec = pl.BlockSpec((None, 1, d), lambda i: (i // tiles_per_batch, 0, 0))
    return [pl.BlockSpec((1, d), lambda i: (0, 0)), vec, vec]


def _adaln_kernel(cv_ref, w_ref, b_ref, o_ref):
    cv = cv_ref[...]
    a = (cv * jax.nn.sigmoid(cv)).astype(BF16)
    o_ref[...] = _dot(a, w_ref[...].astype(BF16)) + b_ref[...]


def _adaln(cv, w_ada, b_ada):
    depth, d, n = w_ada.shape
    tn = 1024
    return pl.pallas_call(
        _adaln_kernel,
        grid=(depth, n // tn),
        in_specs=[
            pl.BlockSpec((8, d), lambda l, j: (0, 0)),
            pl.BlockSpec((None, d, tn), lambda l, j: (l, 0, j)),
            pl.BlockSpec((None, 1, tn), lambda l, j: (l, 0, j)),
        ],
        out_specs=pl.BlockSpec((None, 8, tn), lambda l, j: (l, 0, j)),
        out_shape=jax.ShapeDtypeStruct((depth, 8, n), F32),
        compiler_params=_cparams("parallel", "arbitrary"),
        name="adaln",
    )(cv, w_ada, b_ada.reshape(depth, 1, n))


def _nmm_kernel(x_ref, g_ref, sh_ref, sc_ref, w_ref, b_ref, o_ref, h_ref):
    @pl.when(pl.program_id(1) == 0)
    def _():
        h_ref[...] = _norm_mod(x_ref[...], g_ref[...], sh_ref[...], sc_ref[...]).astype(BF16)

    o_ref[...] = _dot(h_ref[...], w_ref[...]) + b_ref[...]


def _nmm(x, g, shift, scale, w, b, *, tm, tn):
    m, d = x.shape
    n = w.shape[1]
    tiles_per_batch = m // shift.shape[0] // tm
    vec = pl.BlockSpec((None, 1, d), lambda i, j: (i // tiles_per_batch, 0, 0))
    return pl.pallas_call(
        _nmm_kernel,
        grid=(m // tm, n // tn),
        in_specs=[
            pl.BlockSpec((tm, d), lambda i, j: (i, 0)),
            pl.BlockSpec((1, d), lambda i, j: (0, 0)),
            vec, vec,
            pl.BlockSpec((d, tn), lambda i, j: (0, j)),
            pl.BlockSpec((1, tn), lambda i, j: (0, j)),
        ],
        out_specs=pl.BlockSpec((tm, tn), lambda i, j: (i, j)),
        out_shape=jax.ShapeDtypeStruct((m, n), F32),
        scratch_shapes=[pltpu.VMEM((tm, d), BF16)],
        compiler_params=_cparams("parallel", "arbitrary"),
        name="norm_mod_matmul",
    )(x, g, shift, scale, w, b)


_CONV_ROWS = 64


def _abmix_kernel(z_ref, zp_ref, zn_ref, x_ref, gate_ref, alng_ref, alnb_ref, ws_ref, bs_ref,
                  cw_ref, cb_ref, blng_ref, blnb_ref, wout_ref, ng_ref, sh_ref, sc_ref, o_ref, hn_ref,
                  e_ref, hc_ref, y_ref, *, tm, tiles_per_seg):
    i = pl.program_id(0)
    pos = i % tiles_per_seg
    nch = tm // CHUNK

    ga = jax.nn.gelu(z_ref[:, :2 * A_WIDTH], approximate=True)
    u = ga[:, :A_WIDTH]
    vb = _layernorm(ga[:, A_WIDTH:], alng_ref[...], alnb_ref[...]).astype(BF16)
    for h in range(A_HEADS):
        cols = slice(h * CHUNK, (h + 1) * CHUNK)
        rhs = jnp.concatenate([vb[c * CHUNK:(c + 1) * CHUNK, cols] for c in range(nch)], axis=1)
        r = _dot(ws_ref[h], rhs)
        for c in range(nch):
            rows = slice(c * CHUNK, (c + 1) * CHUNK)
            mixed = r[:, c * CHUNK:(c + 1) * CHUNK] + bs_ref[:, cols]
            y_ref[rows, cols] = (u[rows, cols] * mixed).astype(BF16)

    def glu(zz):
        return zz[:, :B_WIDTH] * jax.nn.sigmoid(zz[:, B_WIDTH:])

    def put(r0, val):
        for c in range(B_WIDTH // LANES):
            e_ref[c, r0:r0 + val.shape[0], :] = val[:, c * LANES:(c + 1) * LANES]

    put(0, jnp.where(pos == 0, 0.0, glu(zp_ref[...])))
    put(HALO, glu(z_ref[:, 2 * A_WIDTH:]))
    put(HALO + tm, jnp.where(pos == tiles_per_seg - 1, 0.0, glu(zn_ref[...])))

    base = HALO - B_PAD
    for c in range(B_WIDTH // LANES):
        lanes = slice(c * LANES, (c + 1) * LANES)
        wk = [cw_ref[k:k + 1, lanes] for k in range(B_CONV)]
        bias = cb_ref[:, lanes]

        def row_block(rb, carry, c=c, lanes=lanes, wk=wk, bias=bias):
            r0 = pl.multiple_of(rb * _CONV_ROWS, _CONV_ROWS)
            acc = e_ref[c, pl.ds(r0 + base, _CONV_ROWS), :] * wk[0]
            for k in range(1, B_CONV):
                acc = acc + e_ref[c, pl.ds(r0 + base + k, _CONV_ROWS), :] * wk[k]
            hc_ref[pl.ds(r0, _CONV_ROWS), lanes] = acc + bias
            return carry

        lax.fori_loop(0, tm // _CONV_ROWS, row_block, 0)

    hb = _layernorm(hc_ref[...], blng_ref[...], blnb_ref[...])
    y_ref[:, A_WIDTH:] = (hb * jax.nn.sigmoid(hb)).astype(BF16)

    y = x_ref[...] + gate_ref[...] * _dot(y_ref[...], wout_ref[...])
    o_ref[...] = y
    hn_ref[...] = _norm_mod(y, ng_ref[...], sh_ref[...], sc_ref[...]).astype(BF16)


def _abmix(z, x, gate, p, nxt, *, tm, seg_len):
    m, d = x.shape
    tiles_per_seg = seg_len // tm
    tiles_per_batch = m // gate.shape[0] // tm
    hb = tm // HALO
    nhalo = m // HALO
    kern = functools.partial(_abmix_kernel, tm=tm, tiles_per_seg=tiles_per_seg)
    row = lambda w: pl.BlockSpec((1, w), lambda i: (0, 0))
    tile = pl.BlockSpec((tm, d), lambda i: (i, 0))
    return pl.pallas_call(
        kern,
        grid=(m // tm,),
        in_specs=[
            pl.BlockSpec((tm, 2 * A_WIDTH + 2 * B_WIDTH), lambda i: (i, 0)),
            pl.BlockSpec((HALO, 2 * B_WIDTH), lambda i: (jnp.maximum(i * hb - 1, 0), 1)),
            pl.BlockSpec((HALO, 2 * B_WIDTH), lambda i: (jnp.minimum((i + 1) * hb, nhalo - 1), 1)),
            tile,
            pl.BlockSpec((None, 1, d), lambda i: (i // tiles_per_batch, 0, 0)),
            row(A_WIDTH), row(A_WIDTH),
            _const_spec((A_HEADS, CHUNK, CHUNK)),
            _const_spec((CHUNK, A_WIDTH)),
            _const_spec((B_CONV, B_WIDTH)),
            row(B_WIDTH), row(B_WIDTH), row(B_WIDTH),
            _const_spec((A_WIDTH + B_WIDTH, d)),
            *_next_specs(d, tiles_per_batch),
        ],
        out_specs=[tile, tile],
        out_shape=[jax.ShapeDtypeStruct((m, d), F32), jax.ShapeDtypeStruct((m, d), BF16)],
        scratch_shapes=[
            pltpu.VMEM((B_WIDTH // LANES, tm + 2 * HALO, LANES), F32),
            pltpu.VMEM((tm, B_WIDTH), F32),
            pltpu.VMEM((tm, A_WIDTH + B_WIDTH), BF16),
        ],
        compiler_params=_cparams("parallel"),
        name="ab_mixer",
    )(z, z, z, x, gate, p["a_ln_g"], p["a_ln_b"], p["w_s"], p["b_s_full"], p["conv_w"], p["conv_b"],
      p["b_ln_g"], p["b_ln_b"], p["w_out"], *nxt)


def _ffn_up_kernel(h_ref, hp_ref, hn_ref, wg_ref, wu_ref, cw_ref, cb_ref, a_ref, hx_ref, g_ref, *, tm, seg_len):
    i = pl.program_id(0)
    f = pl.program_id(1)
    tiles_per_seg = max(seg_len // tm, 1)
    pos = i % tiles_per_seg

    @pl.when(f == 0)
    def _():
        halo_zeros = jnp.zeros(hp_ref.shape, BF16)
        hx_ref[0:HALO, :] = jnp.where(pos == 0, halo_zeros, hp_ref[...])
        hx_ref[HALO:HALO + tm, :] = h_ref[...]
        hx_ref[HALO + tm:, :] = jnp.where(pos == tiles_per_seg - 1, halo_zeros, hn_ref[...])

    g_ref[...] = _dot(hx_ref[...], wg_ref[...].astype(BF16))
    u = _dot(hx_ref[HALO:HALO + tm, :], wu_ref[...].astype(BF16))
    g_prev = g_ref[pl.ds(HALO - 1, tm), :]
    g_next = g_ref[pl.ds(HALO + 1, tm), :]
    if tm > seg_len:
        r = lax.broadcasted_iota(jnp.int32, g_prev.shape, 0) % seg_len
        g_prev = jnp.where(r == 0, 0.0, g_prev)
        g_next = jnp.where(r == seg_len - 1, 0.0, g_next)
    gc = (cw_ref[0:1, :] * g_prev + cw_ref[1:2, :] * g_ref[pl.ds(HALO, tm), :]
          + cw_ref[2:3, :] * g_next + cb_ref[...])
    a_ref[...] = (gc * jax.nn.sigmoid(gc) * u).astype(BF16)


def _ffn_down_kernel(a_ref, x_ref, gate_ref, wd_ref, ng_ref, *rest, final):
    y = x_ref[...] + gate_ref[...] * _dot(a_ref[...], wd_ref[...])
    if final:
        (o_ref,) = rest
        o_ref[...] = _rms(y, ng_ref[...])
    else:
        sh_ref, sc_ref, o_ref, hn_ref = rest
        o_ref[...] = y
        hn_ref[...] = _norm_mod(y, ng_ref[...], sh_ref[...], sc_ref[...]).astype(BF16)


def _ffn(h, x, gate, w_up, conv_w, conv_b, w_down, nxt, *, layer, tm_up, tf, tm_down, seg_len, final):
    m, d = x.shape
    nf = D_FF // tf
    hb = tm_up // HALO
    nhalo = m // HALO
    assert seg_len % tm_up == 0 or tm_up % seg_len == 0
    kern = functools.partial(_ffn_up_kernel, tm=tm_up, seg_len=seg_len)
    a = pl.pallas_call(
        kern,
        grid=(m // tm_up, nf),
        in_specs=[
            pl.BlockSpec((tm_up, d), lambda i, f: (i, 0)),
            pl.BlockSpec((HALO, d), lambda i, f: (jnp.maximum(i * hb - 1, 0), 0)),
            pl.BlockSpec((HALO, d), lambda i, f: (jnp.minimum((i + 1) * hb, nhalo - 1), 0)),
            pl.BlockSpec((None, d, tf), lambda i, f: (layer, 0, f)),
            pl.BlockSpec((None, d, tf), lambda i, f: (layer, 0, nf + f)),
            pl.BlockSpec((None, 3, tf), lambda i, f: (layer, 0, f)),
            pl.BlockSpec((None, 1, tf), lambda i, f: (layer, 0, f)),
        ],
        out_specs=pl.BlockSpec((tm_up, tf), lambda i, f: (i, f)),
        out_shape=jax.ShapeDtypeStruct((m, D_FF), BF16),
        scratch_shapes=[
            pltpu.VMEM((tm_up + 2 * HALO, d), BF16),
            pltpu.VMEM((tm_up + 2 * HALO, tf), F32),
        ],
        compiler_params=_cparams("parallel", "arbitrary"),
        name="ffn_up_conv_gate",
    )(h, h, h, w_up, w_up, conv_w, conv_b)

    tiles_per_batch = m // gate.shape[0] // tm_down
    tile = pl.BlockSpec((tm_down, d), lambda i: (i, 0))
    return pl.pallas_call(
        functools.partial(_ffn_down_kernel, final=final),
        grid=(m // tm_down,),
        in_specs=[
            pl.BlockSpec((tm_down, D_FF), lambda i: (i, 0)),
            tile,
            pl.BlockSpec((None, 1, d), lambda i: (i // tiles_per_batch, 0, 0)),
            pl.BlockSpec((None, D_FF, d), lambda i: (layer, 0, 0), pipeline_mode=pl.Buffered(1)),
            *_next_specs(d, tiles_per_batch)[:1 if final else 3],
        ],
        out_specs=tile if final else [tile, tile],
        out_shape=(jax.ShapeDtypeStruct((m, d), F32) if final else
                   [jax.ShapeDtypeStruct((m, d), F32), jax.ShapeDtypeStruct((m, d), BF16)]),
        compiler_params=_cparams("parallel"),
        name="ffn_down",
    )(a, x, gate, w_down, *nxt)


_ZPAD = Q_LORA + KV_LORA + LANES


def _rope_tile(xt, c, s1, s2, odd):
    if odd:
        return pltpu.roll(xt, 64, 1) * c + pltpu.roll(xt, 32, 1) * s1 + pltpu.roll(xt, 96, 1) * s2
    return xt * c + pltpu.roll(xt, 96, 1) * s1 + pltpu.roll(xt, 32, 1) * s2


def _mla_proj_kernel(hl_ref, hc_ref, win_ref, qg_ref, kvg_ref, wqn_ref, wqp_ref,
                     wuk_ref, wuv_ref, c_ref, s1_ref, s2_ref, q_ref, k_ref, v_ref, *, n_lat_tiles, q_scale):
    t = pl.program_id(1)
    is_ctx = t >= n_lat_tiles
    h = jnp.where(is_ctx, hc_ref[...], hl_ref[...])
    z = _dot(h, win_ref[...])
    c, s1, s2 = c_ref[...], s1_ref[...], s2_ref[...]

    ckvn = _rms(z[:, Q_LORA:Q_LORA + KV_LORA], kvg_ref[...]).astype(BF16)
    kpe = _rope_tile(z[:, Q_LORA + KV_LORA:], c, s1, s2, False).astype(BF16)
    kn = _dot(ckvn, wuk_ref[...])
    v_ref[...] = lax.dot_general(wuv_ref[...], ckvn, (((1,), (1,)), ((), ())),
                                 preferred_element_type=F32).astype(BF16)
    for hd in range(MLA_HEADS):
        k_ref[:, hd * HEAD_PAD:hd * HEAD_PAD + QK_NOPE] = kn[:, hd * QK_NOPE:(hd + 1) * QK_NOPE].astype(BF16)
        k_ref[:, hd * HEAD_PAD + QK_NOPE:(hd + 1) * HEAD_PAD] = kpe

    @pl.when(jnp.logical_not(is_ctx))
    def _():
        cqn = _rms(z[:, :Q_LORA], qg_ref[...]).astype(BF16)
        qn = _dot(cqn, wqn_ref[...])
        qp = _dot(cqn, wqp_ref[...])
        for hd in range(MLA_HEADS):
            q_ref[:, hd * HEAD_PAD:hd * HEAD_PAD + QK_NOPE] = (
                qn[:, hd * QK_NOPE:(hd + 1) * QK_NOPE] * q_scale).astype(BF16)
            xt = qp[:, (hd // 2) * LANES:(hd // 2 + 1) * LANES]
            q_ref[:, hd * HEAD_PAD + QK_NOPE:(hd + 1) * HEAD_PAD] = (
                _rope_tile(xt, c, s1, s2, hd % 2 == 1) * q_scale).astype(BF16)


def _mla_proj(hl, hc, p, tabs):
    bsz, seq, d = hl.shape
    lc = hc.shape[1]
    tm = MLA_TM
    assert lc == tm
    nlt = seq // tm
    nt = nlt + 1
    kern = functools.partial(_mla_proj_kernel, n_lat_tiles=nlt,
                             q_scale=float((QK_NOPE + QK_ROPE) ** -0.5 * LOG2_E))
    tab = pl.BlockSpec((tm, LANES), lambda b, t: (t, 0))
    hw = MLA_HEADS * HEAD_PAD
    return pl.pallas_call(
        kern,
        grid=(bsz, nt),
        in_specs=[
            pl.BlockSpec((None, tm, d), lambda b, t: (b, jnp.minimum(t, nlt - 1), 0)),
            pl.BlockSpec((None, tm, d), lambda b, t: (b, 0, 0)),
            _const_spec((d, _ZPAD)),
            pl.BlockSpec((1, Q_LORA), lambda b, t: (0, 0)),
            pl.BlockSpec((1, KV_LORA), lambda b, t: (0, 0)),
            _const_spec((Q_LORA, MLA_HEADS * QK_NOPE)),
            _const_spec((Q_LORA, MLA_HEADS * QK_ROPE)),
            _const_spec((KV_LORA, MLA_HEADS * QK_NOPE)),
            _const_spec((MLA_HEADS * V_DIM, KV_LORA)),
            tab, tab, tab,
        ],
        out_specs=[
            pl.BlockSpec((None, tm, hw), lambda b, t: (b, jnp.minimum(t, nlt - 1), 0)),
            pl.BlockSpec((None, tm, hw), lambda b, t: (b, t, 0)),
            pl.BlockSpec((None, MLA_HEADS * V_DIM, tm), lambda b, t: (b, 0, t)),
        ],
        out_shape=[
            jax.ShapeDtypeStruct((bsz, seq, hw), BF16),
            jax.ShapeDtypeStruct((bsz, seq + lc, hw), BF16),
            jax.ShapeDtypeStruct((bsz, MLA_HEADS * V_DIM, seq + lc), BF16),
        ],
        compiler_params=_cparams("parallel", "arbitrary"),
        name="mla_proj",
    )(hl, hc, p["w_in"], p["q_norm_g"], p["kv_norm_g"], p["w_uq_n"], p["w_uq_p"],
      p["w_uk"], p["w_uv"], *tabs)


def _key_chunks(lk, n):
    tiles = lk // HEAD_PAD
    assert tiles * HEAD_PAD == lk and tiles >= n
    sizes = [(tiles // n + (1 if c >= n - tiles % n else 0)) * HEAD_PAD for c in range(n)]
    starts = [sum(sizes[:c]) for c in range(n)]
    return list(zip(starts, sizes))


def _attn_kernel(q_ref, k_ref, vt_ref, o_ref, *, chunks, heads):
    items = [(j, c) for j in range(heads) for c in range(len(chunks))]

    def scores(j, c):
        start, size = chunks[c]
        return lax.dot_general(k_ref[start:start + size, j * HEAD_PAD:(j + 1) * HEAD_PAD],
                               q_ref[:, j * HEAD_PAD:(j + 1) * HEAD_PAD], (((1,), (1,)), ((), ())),
                               preferred_element_type=F32)

    m = l = acc = None
    s_next = scores(*items[0])
    for n, (j, c) in enumerate(items):
        start, size = chunks[c]
        s = s_next
        if n + 1 < len(items):
            s_next = scores(*items[n + 1])
        mc = jnp.max(s, axis=0, keepdims=True)
        m_new = mc if c == 0 else jnp.maximum(m, mc)
        p = jnp.exp2(s - m_new)
        lc = jnp.sum(p, axis=0, keepdims=True)
        pv = _dot(vt_ref[j * V_DIM:(j + 1) * V_DIM, start:start + size], p.astype(BF16))
        if c == 0:
            l, acc = lc, pv
        else:
            alpha = jnp.exp2(m - m_new)
            l = alpha * l + lc
            acc = alpha * acc + pv
        m = m_new
        if c == len(chunks) - 1:
            o_ref[:, j * V_DIM:(j + 1) * V_DIM] = (acc / l).T.astype(o_ref.dtype)


def _attention(q, k, v, *, tq, n_chunks, heads):
    bsz, lq, _ = q.shape
    lk = k.shape[1]
    return pl.pallas_call(
        functools.partial(_attn_kernel, chunks=_key_chunks(lk, n_chunks), heads=heads),
        grid=(bsz, MLA_HEADS // heads, lq // tq),
        in_specs=[
            pl.BlockSpec((None, tq, heads * HEAD_PAD), lambda b, h, i: (b, i, h)),
            pl.BlockSpec((None, lk, heads * HEAD_PAD), lambda b, h, i: (b, 0, h)),
            pl.BlockSpec((None, heads * V_DIM, lk), lambda b, h, i: (b, h, 0)),
        ],
        out_specs=pl.BlockSpec((None, tq, heads * V_DIM), lambda b, h, i: (b, i, h)),
        out_shape=jax.ShapeDtypeStruct((bsz, lq, MLA_HEADS * V_DIM), BF16),
        compiler_params=_cparams("parallel", "parallel", "arbitrary"),
        name="mla_attention",
    )(q, k, v)


def _oproj_kernel(a_ref, x_ref, gate_ref, w_ref, ng_ref, sh_ref, sc_ref, o_ref, hn_ref):
    y = x_ref[...] + gate_ref[...] * _dot(a_ref[...], w_ref[...])
    o_ref[...] = y
    hn_ref[...] = _norm_mod(y, ng_ref[...], sh_ref[...], sc_ref[...]).astype(BF16)


def _oproj(a, x, gate, w, nxt, *, tm):
    m, d = x.shape
    kdim = a.shape[1]
    tiles_per_batch = m // gate.shape[0] // tm
    tile = pl.BlockSpec((tm, d), lambda i: (i, 0))
    return pl.pallas_call(
        _oproj_kernel,
        grid=(m // tm,),
        in_specs=[
            pl.BlockSpec((tm, kdim), lambda i: (i, 0)),
            tile,
            pl.BlockSpec((None, 1, d), lambda i: (i // tiles_per_batch, 0, 0)),
            _const_spec((kdim, d)),
            *_next_specs(d, tiles_per_batch),
        ],
        out_specs=[tile, tile],
        out_shape=[jax.ShapeDtypeStruct((m, d), F32), jax.ShapeDtypeStruct((m, d), BF16)],
        compiler_params=_cparams("parallel"),
        name="attn_out_proj",
    )(a, x, gate, w, *nxt)


def _rope_tables(seq, ctx_len):
    rows = seq // GRID_W
    row = jnp.repeat(jnp.arange(rows, dtype=F32), GRID_W)
    col = jnp.tile(jnp.arange(GRID_W, dtype=F32), rows)
    n_freq = QK_ROPE // 4
    inv = ROPE_THETA ** (-jnp.arange(n_freq, dtype=F32) / n_freq)
    ang = jnp.concatenate([row[:, None] * inv, col[:, None] * inv], axis=-1)
    cos, sin = jnp.cos(ang), jnp.sin(ang)
    half = QK_ROPE // 2
    cos = jnp.concatenate([cos, jnp.ones((ctx_len, half), F32)], axis=0)
    sin = jnp.concatenate([sin, jnp.zeros((ctx_len, half), F32)], axis=0)
    zh = jnp.zeros_like(cos)
    c = jnp.concatenate([cos, cos, zh, zh], axis=-1)
    s1 = jnp.concatenate([-sin, zh, zh, zh], axis=-1)
    s2 = jnp.concatenate([zh, sin, zh, zh], axis=-1)
    return c, s1, s2


def kernel(x, c, ctx, c_ctx, norm1_g, norm2_g, w_ada, b_ada, ab_w_in, ab_b_in, a_ln_g, a_ln_b, a_w_s, a_b_s, b_conv_w, b_conv_b, b_ln_g, b_ln_b, ab_w_out, mla_w_in, mla_q_norm_g, mla_w_uq, mla_kv_norm_g, mla_w_ukv, mla_w_o, ffn_w_up, ffn_conv_w, ffn_conv_b, ffn_w_down, final_norm_g):
    bsz, seq, d = x.shape
    lc = ctx.shape[1]
    depth = w_ada.shape[0]
    assert depth == 2, "the call chain below is written for one even (mixer) layer followed by one MLA layer"

    cv = jnp.concatenate([c, c_ctx[None, :], jnp.zeros((8 - bsz - 1, d), F32)], axis=0)
    mods = _adaln(cv, w_ada, b_ada)

    def lat_mod(i, k):
        return mods[i, :bsz, k * d:(k + 1) * d].reshape(bsz, 1, d)

    def ctx_mod(i, k):
        return mods[i, bsz:bsz + 1, k * d:(k + 1) * d].reshape(1, 1, d)

    ffn_w = (ffn_w_up, ffn_conv_w, ffn_conv_b.reshape(depth, 1, -1), ffn_w_down.astype(BF16))
    xl = x.reshape(bsz * seq, d)
    xc = ctx.reshape(bsz * lc, d)
    n_ctx_rows = bsz * lc

    p = {
        "a_ln_g": a_ln_g[0].reshape(1, -1), "a_ln_b": a_ln_b[0].reshape(1, -1),
        "w_s": a_w_s[0].astype(BF16),
        "b_s_full": jnp.repeat(a_b_s[0].T, A_WIDTH // A_HEADS, axis=1),
        "conv_w": b_conv_w[0], "conv_b": b_conv_b[0].reshape(1, -1),
        "b_ln_g": b_ln_g[0].reshape(1, -1), "b_ln_b": b_ln_b[0].reshape(1, -1),
        "w_out": ab_w_out[0].astype(BF16),
    }
    w_in = ab_w_in[0].astype(BF16)
    b_in = ab_b_in[0].reshape(1, -1)
    ng1, ng2 = norm1_g[0].reshape(1, d), norm2_g[0].reshape(1, d)
    ng1_next = norm1_g[1].reshape(1, d)

    z = _nmm(xl, ng1, lat_mod(0, 0), lat_mod(0, 1), w_in, b_in, tm=NMM_TM, tn=NMM_TN)
    xl, hl = _abmix(z, xl, lat_mod(0, 2), p, (ng2, lat_mod(0, 3), lat_mod(0, 4)), tm=ABMIX_TM, seg_len=seq)
    xl, hl = _ffn(hl, xl, lat_mod(0, 5), *ffn_w, (ng1_next, lat_mod(1, 0), lat_mod(1, 1)), layer=0,
                  tm_up=FFN_UP_TM, tf=FFN_TF, tm_down=FFN_DOWN_TM, seg_len=seq, final=False)

    zc = _nmm(xc, ng1, ctx_mod(0, 0), ctx_mod(0, 1), w_in, b_in, tm=n_ctx_rows, tn=NMM_TN)
    xc, hc = _abmix(zc, xc, ctx_mod(0, 2), p, (ng2, ctx_mod(0, 3), ctx_mod(0, 4)), tm=CTX_TM, seg_len=lc)
    xc, hc = _ffn(hc, xc, ctx_mod(0, 5), *ffn_w, (ng1_next, ctx_mod(1, 0), ctx_mod(1, 1)), layer=0,
                  tm_up=n_ctx_rows, tf=FFN_TF, tm_down=CTX_TM, seg_len=lc, final=False)

    w_uq = mla_w_uq[0].reshape(Q_LORA, MLA_HEADS, QK_NOPE + QK_ROPE)
    w_ukv = mla_w_ukv[0].reshape(KV_LORA, MLA_HEADS, QK_NOPE + V_DIM)
    p = {
        "w_in": jnp.pad(mla_w_in[0], ((0, 0), (0, LANES - QK_ROPE))).astype(BF16),
        "q_norm_g": mla_q_norm_g[0].reshape(1, -1), "kv_norm_g": mla_kv_norm_g[0].reshape(1, -1),
        "w_uq_n": w_uq[:, :, :QK_NOPE].reshape(Q_LORA, -1).astype(BF16),
        "w_uq_p": w_uq[:, :, QK_NOPE:].reshape(Q_LORA, -1).astype(BF16),
        "w_uk": w_ukv[:, :, :QK_NOPE].reshape(KV_LORA, -1).astype(BF16),
        "w_uv": w_ukv[:, :, QK_NOPE:].reshape(KV_LORA, -1).T.astype(BF16),
    }
    q, k, v = _mla_proj(hl.reshape(bsz, seq, d), hc.reshape(bsz, lc, d), p, _rope_tables(seq, lc))
    a = _attention(q, k, v, tq=ATTN_TQ, n_chunks=ATTN_KEY_CHUNKS, heads=ATTN_HEADS_PER_STEP)
    xl, hl = _oproj(a.reshape(bsz * seq, -1), xl, lat_mod(1, 2), mla_w_o[0].astype(BF16),
                    (norm2_g[1].reshape(1, d), lat_mod(1, 3), lat_mod(1, 4)), tm=OPROJ_TM)
    out = _ffn(hl, xl, lat_mod(1, 5), *ffn_w, (final_norm_g.reshape(1, d),), layer=1,
               tm_up=FFN_UP_TM, tf=FFN_TF, tm_down=FFN_DOWN_TM, seg_len=seq, final=True)
    return out.reshape(bsz, seq, d)
```

```python
import functools

import jax
import jax.numpy as jnp
from jax import lax
from jax.experimental import pallas as pl
from jax.experimental.pallas import tpu as pltpu

F32 = jnp.float32
BF16 = jnp.bfloat16

D_MODEL = 2048
GRID_W = 64
EPS = 1e-6
N_MOD = 6
CHUNK = 128
A_HEADS = 8
A_WIDTH = 1024
B_WIDTH = 1024
B_CONV = 31
MLA_HEADS = 16
Q_LORA = 768
KV_LORA = 512
QK_NOPE = 128
QK_ROPE = 64
V_DIM = 128
ROPE_THETA = 10000.0
D_FF = 5632
LOG2_E = 1.4426950408889634

LANES = 128
BF16_SUBLANES = 16
HEAD_PAD = 256
VMEM_LIMIT = 56 * 2**20

HALO = BF16_SUBLANES
B_PAD = (B_CONV - 1) // 2

NMM_TM, NMM_TN = 1024, 1024
ABMIX_TM = 256
FFN_UP_TM, FFN_TF = 1024, 512
FFN_DOWN_TM = 256
MLA_TM = 256
ATTN_TQ, ATTN_KEY_CHUNKS, ATTN_HEADS_PER_STEP = 1024, 3, 4
OPROJ_TM = 512
CTX_TM = 256


def _cparams(*sem):
    return pltpu.CompilerParams(dimension_semantics=sem, vmem_limit_bytes=VMEM_LIMIT)


def _const_spec(shape):
    n = len(shape)
    return pl.BlockSpec(shape, lambda *_: (0,) * n, pipeline_mode=pl.Buffered(1))


def _rms(x, g):
    return x * lax.rsqrt(jnp.mean(x * x, axis=-1, keepdims=True) + EPS) * g


def _layernorm(x, g, b):
    mu = jnp.mean(x, axis=-1, keepdims=True)
    xc = x - mu
    var = jnp.mean(xc * xc, axis=-1, keepdims=True)
    return xc * lax.rsqrt(var + EPS) * g + b


def _norm_mod(x, g, shift, scale):
    return _rms(x, g) * (1.0 + scale) + shift


def _dot(a, b):
    return jnp.dot(a, b, preferred_element_type=F32)


def _next_specs(d, tiles_per_batch):
    vec = pl.BlockSpec((None, 1, d), lambda i: (i // tiles_per_batch, 0, 0))
    return [pl.BlockSpec((1, d), lambda i: (0, 0)), vec, vec]


def _adaln_kernel(cv_ref, w_ref, b_ref, o_ref):
    cv = cv_ref[...]
    a = (cv * jax.nn.sigmoid(cv)).astype(BF16)
    o_ref[...] = _dot(a, w_ref[...].astype(BF16)) + b_ref[...]


def _adaln(cv, w_ada, b_ada):
    depth, d, n = w_ada.shape
    tn = 1024
    return pl.pallas_call(
        _adaln_kernel,
        grid=(depth, n // tn),
        in_specs=[
            pl.BlockSpec((8, d), lambda l, j: (0, 0)),
            pl.BlockSpec((None, d, tn), lambda l, j: (l, 0, j)),
            pl.BlockSpec((None, 1, tn), lambda l, j: (l, 0, j)),
        ],
        out_specs=pl.BlockSpec((None, 8, tn), lambda l, j: (l, 0, j)),
        out_shape=jax.ShapeDtypeStruct((depth, 8, n), F32),
        compiler_params=_cparams("parallel", "arbitrary"),
        name="adaln",
    )(cv, w_ada, b_ada.reshape(depth, 1, n))


def _nmm_kernel(x_ref, g_ref, sh_ref, sc_ref, w_ref, b_ref, o_ref, h_ref):
    @pl.when(pl.program_id(1) == 0)
    def _():
        h_ref[...] = _norm_mod(x_ref[...], g_ref[...], sh_ref[...], sc_ref[...]).astype(BF16)

    o_ref[...] = _dot(h_ref[...], w_ref[...]) + b_ref[...]


def _nmm(x, g, shift, scale, w, b, *, tm, tn):
    m, d = x.shape
    n = w.shape[1]
    tiles_per_batch = m // shift.shape[0] // tm
    vec = pl.BlockSpec((None, 1, d), lambda i, j: (i // tiles_per_batch, 0, 0))
    return pl.pallas_call(
        _nmm_kernel,
        grid=(m // tm, n // tn),
        in_specs=[
            pl.BlockSpec((tm, d), lambda i, j: (i, 0)),
            pl.BlockSpec((1, d), lambda i, j: (0, 0)),
            vec, vec,
            pl.BlockSpec((d, tn), lambda i, j: (0, j)),
            pl.BlockSpec((1, tn), lambda i, j: (0, j)),
        ],
        out_specs=pl.BlockSpec((tm, tn), lambda i, j: (i, j)),
        out_shape=jax.ShapeDtypeStruct((m, n), F32),
        scratch_shapes=[pltpu.VMEM((tm, d), BF16)],
        compiler_params=_cparams("parallel", "arbitrary"),
        name="norm_mod_matmul",
    )(x, g, shift, scale, w, b)


_CONV_ROWS = 64


def _abmix_kernel(z_ref, zp_ref, zn_ref, x_ref, gate_ref, alng_ref, alnb_ref, ws_ref, bs_ref,
                  cw_ref, cb_ref, blng_ref, blnb_ref, wout_ref, ng_ref, sh_ref, sc_ref, o_ref, hn_ref,
                  e_ref, hc_ref, y_ref, *, tm, tiles_per_seg):
    i = pl.program_id(0)
    pos = i % tiles_per_seg
    nch = tm // CHUNK

    ga = jax.nn.gelu(z_ref[:, :2 * A_WIDTH], approximate=True)
    u = ga[:, :A_WIDTH]
    vb = _layernorm(ga[:, A_WIDTH:], alng_ref[...], alnb_ref[...]).astype(BF16)
    for h in range(A_HEADS):
        cols = slice(h * CHUNK, (h + 1) * CHUNK)
        rhs = jnp.concatenate([vb[c * CHUNK:(c + 1) * CHUNK, cols] for c in range(nch)], axis=1)
        r = _dot(ws_ref[h], rhs)
        for c in range(nch):
            rows = slice(c * CHUNK, (c + 1) * CHUNK)
            mixed = r[:, c * CHUNK:(c + 1) * CHUNK] + bs_ref[:, cols]
            y_ref[rows, cols] = (u[rows, cols] * mixed).astype(BF16)

    def glu(zz):
        return zz[:, :B_WIDTH] * jax.nn.sigmoid(zz[:, B_WIDTH:])

    def put(r0, val):
        for c in range(B_WIDTH // LANES):
            e_ref[c, r0:r0 + val.shape[0], :] = val[:, c * LANES:(c + 1) * LANES]

    put(0, jnp.where(pos == 0, 0.0, glu(zp_ref[...])))
    put(HALO, glu(z_ref[:, 2 * A_WIDTH:]))
    put(HALO + tm, jnp.where(pos == tiles_per_seg - 1, 0.0, glu(zn_ref[...])))

    base = HALO - B_PAD
    for c in range(B_WIDTH // LANES):
        lanes = slice(c * LANES, (c + 1) * LANES)
        wk = [cw_ref[k:k + 1, lanes] for k in range(B_CONV)]
        bias = cb_ref[:, lanes]

        def row_block(rb, carry, c=c, lanes=lanes, wk=wk, bias=bias):
            r0 = pl.multiple_of(rb * _CONV_ROWS, _CONV_ROWS)
            acc = e_ref[c, pl.ds(r0 + base, _CONV_ROWS), :] * wk[0]
            for k in range(1, B_CONV):
                acc = acc + e_ref[c, pl.ds(r0 + base + k, _CONV_ROWS), :] * wk[k]
            hc_ref[pl.ds(r0, _CONV_ROWS), lanes] = acc + bias
            return carry

        lax.fori_loop(0, tm // _CONV_ROWS, row_block, 0)

    hb = _layernorm(hc_ref[...], blng_ref[...], blnb_ref[...])
    y_ref[:, A_WIDTH:] = (hb * jax.nn.sigmoid(hb)).astype(BF16)

    y = x_ref[...] + gate_ref[...] * _dot(y_ref[...], wout_ref[...])
    o_ref[...] = y
    hn_ref[...] = _norm_mod(y, ng_ref[...], sh_ref[...], sc_ref[...]).astype(BF16)


def _abmix(z, x, gate, p, nxt, *, tm, seg_len):
    m, d = x.shape
    tiles_per_seg = seg_len // tm
    tiles_per_batch = m // gate.shape[0] // tm
    hb = tm // HALO
    nhalo = m // HALO
    kern = functools.partial(_abmix_kernel, tm=tm, tiles_per_seg=tiles_per_seg)
    row = lambda w: pl.BlockSpec((1, w), lambda i: (0, 0))
    tile = pl.BlockSpec((tm, d), lambda i: (i, 0))
    return pl.pallas_call(
        kern,
        grid=(m // tm,),
        in_specs=[
            pl.BlockSpec((tm, 2 * A_WIDTH + 2 * B_WIDTH), lambda i: (i, 0)),
            pl.BlockSpec((HALO, 2 * B_WIDTH), lambda i: (jnp.maximum(i * hb - 1, 0), 1)),
            pl.BlockSpec((HALO, 2 * B_WIDTH), lambda i: (jnp.minimum((i + 1) * hb, nhalo - 1), 1)),
            tile,
            pl.BlockSpec((None, 1, d), lambda i: (i // tiles_per_batch, 0, 0)),
            row(A_WIDTH), row(A_WIDTH),
            _const_spec((A_HEADS, CHUNK, CHUNK)),
            _const_spec((CHUNK, A_WIDTH)),
            _const_spec((B_CONV, B_WIDTH)),
            row(B_WIDTH), row(B_WIDTH), row(B_WIDTH),
            _const_spec((A_WIDTH + B_WIDTH, d)),
            *_next_specs(d, tiles_per_batch),
        ],
        out_specs=[tile, tile],
        out_shape=[jax.ShapeDtypeStruct((m, d), F32), jax.ShapeDtypeStruct((m, d), BF16)],
        scratch_shapes=[
            pltpu.VMEM((B_WIDTH // LANES, tm + 2 * HALO, LANES), F32),
            pltpu.VMEM((tm, B_WIDTH), F32),
            pltpu.VMEM((tm, A_WIDTH + B_WIDTH), BF16),
        ],
        compiler_params=_cparams("parallel"),
        name="ab_mixer",
    )(z, z, z, x, gate, p["a_ln_g"], p["a_ln_b"], p["w_s"], p["b_s_full"], p["conv_w"], p["conv_b"],
      p["b_ln_g"], p["b_ln_b"], p["w_out"], *nxt)


def _ffn_up_kernel(h_ref, hp_ref, hn_ref, wg_ref, wu_ref, cw_ref, cb_ref, a_ref, hx_ref, g_ref, *, tm, seg_len):
    i = pl.program_id(0)
    f = pl.program_id(1)
    tiles_per_seg = max(seg_len // tm, 1)
    pos = i % tiles_per_seg

    @pl.when(f == 0)
    def _():
        halo_zeros = jnp.zeros(hp_ref.shape, BF16)
        hx_ref[0:HALO, :] = jnp.where(pos == 0, halo_zeros, hp_ref[...])
        hx_ref[HALO:HALO + tm, :] = h_ref[...]
        hx_ref[HALO + tm:, :] = jnp.where(pos == tiles_per_seg - 1, halo_zeros, hn_ref[...])

    g = _dot(hx_ref[...], wg_ref[...].astype(BF16))
    u = _dot(hx_ref[HALO:HALO + tm, :], wu_ref[...].astype(BF16))
    for c in range(g.shape[1] // LANES):
        lanes = slice(c * LANES, (c + 1) * LANES)
        g_ref[c] = g[:, lanes]
        g_prev = g_ref[c, pl.ds(HALO - 1, tm), :]
        g_next = g_ref[c, pl.ds(HALO + 1, tm), :]
        if tm > seg_len:
            r = lax.broadcasted_iota(jnp.int32, g_prev.shape, 0) % seg_len
            g_prev = jnp.where(r == 0, 0.0, g_prev)
            g_next = jnp.where(r == seg_len - 1, 0.0, g_next)
        gc = (cw_ref[0:1, lanes] * g_prev + cw_ref[1:2, lanes] * g[HALO:HALO + tm, lanes]
              + cw_ref[2:3, lanes] * g_next + cb_ref[:, lanes])
        a_ref[:, lanes] = (gc * jax.nn.sigmoid(gc) * u[:, lanes]).astype(BF16)


def _ffn_down_kernel(a_ref, x_ref, gate_ref, wd_ref, ng_ref, *rest, final):
    y = x_ref[...] + gate_ref[...] * _dot(a_ref[...], wd_ref[...])
    if final:
        (o_ref,) = rest
        o_ref[...] = _rms(y, ng_ref[...])
    else:
        sh_ref, sc_ref, o_ref, hn_ref = rest
        o_ref[...] = y
        hn_ref[...] = _norm_mod(y, ng_ref[...], sh_ref[...], sc_ref[...]).astype(BF16)


def _ffn(h, x, gate, w_up, conv_w, conv_b, w_down, nxt, *, layer, tm_up, tf, tm_down, seg_len, final):
    m, d = x.shape
    nf = D_FF // tf
    hb = tm_up // HALO
    nhalo = m // HALO
    assert seg_len % tm_up == 0 or tm_up % seg_len == 0
    kern = functools.partial(_ffn_up_kernel, tm=tm_up, seg_len=seg_len)
    a = pl.pallas_call(
        kern,
        grid=(m // tm_up, nf),
        in_specs=[
            pl.BlockSpec((tm_up, d), lambda i, f: (i, 0)),
            pl.BlockSpec((HALO, d), lambda i, f: (jnp.maximum(i * hb - 1, 0), 0)),
            pl.BlockSpec((HALO, d), lambda i, f: (jnp.minimum((i + 1) * hb, nhalo - 1), 0)),
            pl.BlockSpec((None, d, tf), lambda i, f: (layer, 0, f)),
            pl.BlockSpec((None, d, tf), lambda i, f: (layer, 0, nf + f)),
            pl.BlockSpec((None, 3, tf), lambda i, f: (layer, 0, f)),
            pl.BlockSpec((None, 1, tf), lambda i, f: (layer, 0, f)),
        ],
        out_specs=pl.BlockSpec((tm_up, tf), lambda i, f: (i, f)),
        out_shape=jax.ShapeDtypeStruct((m, D_FF), BF16),
        scratch_shapes=[
            pltpu.VMEM((tm_up + 2 * HALO, d), BF16),
            pltpu.VMEM((tf // LANES, tm_up + 2 * HALO, LANES), F32),
        ],
        compiler_params=_cparams("parallel", "arbitrary"),
        name="ffn_up_conv_gate",
    )(h, h, h, w_up, w_up, conv_w, conv_b)

    tiles_per_batch = m // gate.shape[0] // tm_down
    tile = pl.BlockSpec((tm_down, d), lambda i: (i, 0))
    return pl.pallas_call(
        functools.partial(_ffn_down_kernel, final=final),
        grid=(m // tm_down,),
        in_specs=[
            pl.BlockSpec((tm_down, D_FF), lambda i: (i, 0)),
            tile,
            pl.BlockSpec((None, 1, d), lambda i: (i // tiles_per_batch, 0, 0)),
            pl.BlockSpec((None, D_FF, d), lambda i: (layer, 0, 0), pipeline_mode=pl.Buffered(1)),
            *_next_specs(d, tiles_per_batch)[:1 if final else 3],
        ],
        out_specs=tile if final else [tile, tile],
        out_shape=(jax.ShapeDtypeStruct((m, d), F32) if final else
                   [jax.ShapeDtypeStruct((m, d), F32), jax.ShapeDtypeStruct((m, d), BF16)]),
        compiler_params=_cparams("parallel"),
        name="ffn_down",
    )(a, x, gate, w_down, *nxt)


_ZPAD = Q_LORA + KV_LORA + LANES


def _rope_tile(xt, c, s1, s2, odd):
    if odd:
        return pltpu.roll(xt, 64, 1) * c + pltpu.roll(xt, 32, 1) * s1 + pltpu.roll(xt, 96, 1) * s2
    return xt * c + pltpu.roll(xt, 96, 1) * s1 + pltpu.roll(xt, 32, 1) * s2


def _mla_proj_kernel(hl_ref, hc_ref, win_ref, qg_ref, kvg_ref, wqn_ref, wqp_ref,
                     wuk_ref, wuv_ref, c_ref, s1_ref, s2_ref, q_ref, k_ref, v_ref, *, n_lat_tiles, q_scale):
    t = pl.program_id(1)
    is_ctx = t >= n_lat_tiles
    h = jnp.where(is_ctx, hc_ref[...], hl_ref[...])
    z = _dot(h, win_ref[...])
    c, s1, s2 = c_ref[...], s1_ref[...], s2_ref[...]

    ckvn = _rms(z[:, Q_LORA:Q_LORA + KV_LORA], kvg_ref[...]).astype(BF16)
    kpe = _rope_tile(z[:, Q_LORA + KV_LORA:], c, s1, s2, False).astype(BF16)
    kn = _dot(ckvn, wuk_ref[...])
    v_ref[...] = lax.dot_general(wuv_ref[...], ckvn, (((1,), (1,)), ((), ())),
                                 preferred_element_type=F32).astype(BF16)
    for hd in range(MLA_HEADS):
        k_ref[:, hd * HEAD_PAD:hd * HEAD_PAD + QK_NOPE] = kn[:, hd * QK_NOPE:(hd + 1) * QK_NOPE].astype(BF16)
        k_ref[:, hd * HEAD_PAD + QK_NOPE:(hd + 1) * HEAD_PAD] = kpe

    @pl.when(jnp.logical_not(is_ctx))
    def _():
        cqn = _rms(z[:, :Q_LORA], qg_ref[...]).astype(BF16)
        qn = _dot(cqn, wqn_ref[...])
        qp = _dot(cqn, wqp_ref[...])
        for hd in range(MLA_HEADS):
            q_ref[:, hd * HEAD_PAD:hd * HEAD_PAD + QK_NOPE] = (
                qn[:, hd * QK_NOPE:(hd + 1) * QK_NOPE] * q_scale).astype(BF16)
            xt = qp[:, (hd // 2) * LANES:(hd // 2 + 1) * LANES]
            q_ref[:, hd * HEAD_PAD + QK_NOPE:(hd + 1) * HEAD_PAD] = (
                _rope_tile(xt, c, s1, s2, hd % 2 == 1) * q_scale).astype(BF16)


def _mla_proj(hl, hc, p, tabs):
    bsz, seq, d = hl.shape
    lc = hc.shape[1]
    tm = MLA_TM
    assert lc == tm
    nlt = seq // tm
    nt = nlt + 1
    kern = functools.partial(_mla_proj_kernel, n_lat_tiles=nlt,
                             q_scale=float((QK_NOPE + QK_ROPE) ** -0.5 * LOG2_E))
    tab = pl.BlockSpec((tm, LANES), lambda b, t: (t, 0))
    hw = MLA_HEADS * HEAD_PAD
    return pl.pallas_call(
        kern,
        grid=(bsz, nt),
        in_specs=[
            pl.BlockSpec((None, tm, d), lambda b, t: (b, jnp.minimum(t, nlt - 1), 0)),
            pl.BlockSpec((None, tm, d), lambda b, t: (b, 0, 0)),
            _const_spec((d, _ZPAD)),
            pl.BlockSpec((1, Q_LORA), lambda b, t: (0, 0)),
            pl.BlockSpec((1, KV_LORA), lambda b, t: (0, 0)),
            _const_spec((Q_LORA, MLA_HEADS * QK_NOPE)),
            _const_spec((Q_LORA, MLA_HEADS * QK_ROPE)),
            _const_spec((KV_LORA, MLA_HEADS * QK_NOPE)),
            _const_spec((MLA_HEADS * V_DIM, KV_LORA)),
            tab, tab, tab,
        ],
        out_specs=[
            pl.BlockSpec((None, tm, hw), lambda b, t: (b, jnp.minimum(t, nlt - 1), 0)),
            pl.BlockSpec((None, tm, hw), lambda b, t: (b, t, 0)),
            pl.BlockSpec((None, MLA_HEADS * V_DIM, tm), lambda b, t: (b, 0, t)),
        ],
        out_shape=[
            jax.ShapeDtypeStruct((bsz, seq, hw), BF16),
            jax.ShapeDtypeStruct((bsz, seq + lc, hw), BF16),
            jax.ShapeDtypeStruct((bsz, MLA_HEADS * V_DIM, seq + lc), BF16),
        ],
        compiler_params=_cparams("parallel", "arbitrary"),
        name="mla_proj",
    )(hl, hc, p["w_in"], p["q_norm_g"], p["kv_norm_g"], p["w_uq_n"], p["w_uq_p"],
      p["w_uk"], p["w_uv"], *tabs)


def _key_chunks(lk, n):
    tiles = lk // HEAD_PAD
    assert tiles * HEAD_PAD == lk and tiles >= n
    sizes = [(tiles // n + (1 if c >= n - tiles % n else 0)) * HEAD_PAD for c in range(n)]
    starts = [sum(sizes[:c]) for c in range(n)]
    return list(zip(starts, sizes))


def _attn_kernel(q_ref, k_ref, vt_ref, o_ref, *, chunks, heads):
    items = [(j, c) for j in range(heads) for c in range(len(chunks))]

    def scores(j, c):
        start, size = chunks[c]
        return lax.dot_general(k_ref[start:start + size, j * HEAD_PAD:(j + 1) * HEAD_PAD],
                               q_ref[:, j * HEAD_PAD:(j + 1) * HEAD_PAD], (((1,), (1,)), ((), ())),
                               preferred_element_type=F32)

    m = l = acc = None
    s_next = scores(*items[0])
    for n, (j, c) in enumerate(items):
        start, size = chunks[c]
        s = s_next
        if n + 1 < len(items):
            s_next = scores(*items[n + 1])
        mc = jnp.max(s, axis=0, keepdims=True)
        m_new = mc if c == 0 else jnp.maximum(m, mc)
        p = jnp.exp2(s - m_new)
        lc = jnp.sum(p, axis=0, keepdims=True)
        pv = _dot(vt_ref[j * V_DIM:(j + 1) * V_DIM, start:start + size], p.astype(BF16))
        if c == 0:
            l, acc = lc, pv
        else:
            alpha = jnp.exp2(m - m_new)
            l = alpha * l + lc
            acc = alpha * acc + pv
        m = m_new
        if c == len(chunks) - 1:
            o_ref[:, j * V_DIM:(j + 1) * V_DIM] = (acc / l).T.astype(o_ref.dtype)


def _attention(q, k, v, *, tq, n_chunks, heads):
    bsz, lq, _ = q.shape
    lk = k.shape[1]
    return pl.pallas_call(
        functools.partial(_attn_kernel, chunks=_key_chunks(lk, n_chunks), heads=heads),
        grid=(bsz, MLA_HEADS // heads, lq // tq),
        in_specs=[
            pl.BlockSpec((None, tq, heads * HEAD_PAD), lambda b, h, i: (b, i, h)),
            pl.BlockSpec((None, lk, heads * HEAD_PAD), lambda b, h, i: (b, 0, h)),
            pl.BlockSpec((None, heads * V_DIM, lk), lambda b, h, i: (b, h, 0)),
        ],
        out_specs=pl.BlockSpec((None, tq, heads * V_DIM), lambda b, h, i: (b, i, h)),
        out_shape=jax.ShapeDtypeStruct((bsz, lq, MLA_HEADS * V_DIM), BF16),
        compiler_params=_cparams("parallel", "parallel", "arbitrary"),
        name="mla_attention",
    )(q, k, v)


def _oproj_kernel(a_ref, x_ref, gate_ref, w_ref, ng_ref, sh_ref, sc_ref, o_ref, hn_ref):
    y = x_ref[...] + gate_ref[...] * _dot(a_ref[...], w_ref[...])
    o_ref[...] = y
    hn_ref[...] = _norm_mod(y, ng_ref[...], sh_ref[...], sc_ref[...]).astype(BF16)


def _oproj(a, x, gate, w, nxt, *, tm):
    m, d = x.shape
    kdim = a.shape[1]
    tiles_per_batch = m // gate.shape[0] // tm
    tile = pl.BlockSpec((tm, d), lambda i: (i, 0))
    return pl.pallas_call(
        _oproj_kernel,
        grid=(m // tm,),
        in_specs=[
            pl.BlockSpec((tm, kdim), lambda i: (i, 0)),
            tile,
            pl.BlockSpec((None, 1, d), lambda i: (i // tiles_per_batch, 0, 0)),
            _const_spec((kdim, d)),
            *_next_specs(d, tiles_per_batch),
        ],
        out_specs=[tile, tile],
        out_shape=[jax.ShapeDtypeStruct((m, d), F32), jax.ShapeDtypeStruct((m, d), BF16)],
        compiler_params=_cparams("parallel"),
        name="attn_out_proj",
    )(a, x, gate, w, *nxt)


def _rope_tables(seq, ctx_len):
    rows = seq // GRID_W
    row = jnp.repeat(jnp.arange(rows, dtype=F32), GRID_W)
    col = jnp.tile(jnp.arange(GRID_W, dtype=F32), rows)
    n_freq = QK_ROPE // 4
    inv = ROPE_THETA ** (-jnp.arange(n_freq, dtype=F32) / n_freq)
    ang = jnp.concatenate([row[:, None] * inv, col[:, None] * inv], axis=-1)
    cos, sin = jnp.cos(ang), jnp.sin(ang)
    half = QK_ROPE // 2
    cos = jnp.concatenate([cos, jnp.ones((ctx_len, half), F32)], axis=0)
    sin = jnp.concatenate([sin, jnp.zeros((ctx_len, half), F32)], axis=0)
    zh = jnp.zeros_like(cos)
    c = jnp.concatenate([cos, cos, zh, zh], axis=-1)
    s1 = jnp.concatenate([-sin, zh, zh, zh], axis=-1)
    s2 = jnp.concatenate([zh, sin, zh, zh], axis=-1)
    return c, s1, s2


def kernel(x, c, ctx, c_ctx, norm1_g, norm2_g, w_ada, b_ada, ab_w_in, ab_b_in, a_ln_g, a_ln_b, a_w_s, a_b_s, b_conv_w, b_conv_b, b_ln_g, b_ln_b, ab_w_out, mla_w_in, mla_q_norm_g, mla_w_uq, mla_kv_norm_g, mla_w_ukv, mla_w_o, ffn_w_up, ffn_conv_w, ffn_conv_b, ffn_w_down, final_norm_g):
    bsz, seq, d = x.shape
    lc = ctx.shape[1]
    depth = w_ada.shape[0]
    assert depth == 2, "the call chain below is written for one even (mixer) layer followed by one MLA layer"

    cv = jnp.concatenate([c, c_ctx[None, :], jnp.zeros((8 - bsz - 1, d), F32)], axis=0)
    mods = _adaln(cv, w_ada, b_ada)

    def lat_mod(i, k):
        return mods[i, :bsz, k * d:(k + 1) * d].reshape(bsz, 1, d)

    def ctx_mod(i, k):
        return mods[i, bsz:bsz + 1, k * d:(k + 1) * d].reshape(1, 1, d)

    ffn_w = (ffn_w_up, ffn_conv_w, ffn_conv_b.reshape(depth, 1, -1), ffn_w_down.astype(BF16))
    xl = x.reshape(bsz * seq, d)
    xc = ctx.reshape(bsz * lc, d)
    n_ctx_rows = bsz * lc

    p = {
        "a_ln_g": a_ln_g[0].reshape(1, -1), "a_ln_b": a_ln_b[0].reshape(1, -1),
        "w_s": a_w_s[0].astype(BF16),
        "b_s_full": jnp.repeat(a_b_s[0].T, A_WIDTH // A_HEADS, axis=1),
        "conv_w": b_conv_w[0], "conv_b": b_conv_b[0].reshape(1, -1),
        "b_ln_g": b_ln_g[0].reshape(1, -1), "b_ln_b": b_ln_b[0].reshape(1, -1),
        "w_out": ab_w_out[0].astype(BF16),
    }
    w_in = ab_w_in[0].astype(BF16)
    b_in = ab_b_in[0].reshape(1, -1)
    ng1, ng2 = norm1_g[0].reshape(1, d), norm2_g[0].reshape(1, d)
    ng1_next = norm1_g[1].reshape(1, d)

    z = _nmm(xl, ng1, lat_mod(0, 0), lat_mod(0, 1), w_in, b_in, tm=NMM_TM, tn=NMM_TN)
    xl, hl = _abmix(z, xl, lat_mod(0, 2), p, (ng2, lat_mod(0, 3), lat_mod(0, 4)), tm=ABMIX_TM, seg_len=seq)
    xl, hl = _ffn(hl, xl, lat_mod(0, 5), *ffn_w, (ng1_next, lat_mod(1, 0), lat_mod(1, 1)), layer=0,
                  tm_up=FFN_UP_TM, tf=FFN_TF, tm_down=FFN_DOWN_TM, seg_len=seq, final=False)

    zc = _nmm(xc, ng1, ctx_mod(0, 0), ctx_mod(0, 1), w_in, b_in, tm=n_ctx_rows, tn=NMM_TN)
    xc, hc = _abmix(zc, xc, ctx_mod(0, 2), p, (ng2, ctx_mod(0, 3), ctx_mod(0, 4)), tm=CTX_TM, seg_len=lc)
    xc, hc = _ffn(hc, xc, ctx_mod(0, 5), *ffn_w, (ng1_next, ctx_mod(1, 0), ctx_mod(1, 1)), layer=0,
                  tm_up=n_ctx_rows, tf=FFN_TF, tm_down=CTX_TM, seg_len=lc, final=False)

    w_uq = mla_w_uq[0].reshape(Q_LORA, MLA_HEADS, QK_NOPE + QK_ROPE)
    w_ukv = mla_w_ukv[0].reshape(KV_LORA, MLA_HEADS, QK_NOPE + V_DIM)
    p = {
        "w_in": jnp.pad(mla_w_in[0], ((0, 0), (0, LANES - QK_ROPE))).astype(BF16),
        "q_norm_g": mla_q_norm_g[0].reshape(1, -1), "kv_norm_g": mla_kv_norm_g[0].reshape(1, -1),
        "w_uq_n": w_uq[:, :, :QK_NOPE].reshape(Q_LORA, -1).astype(BF16),
        "w_uq_p": w_uq[:, :, QK_NOPE:].reshape(Q_LORA, -1).astype(BF16),
        "w_uk": w_ukv[:, :, :QK_NOPE].reshape(KV_LORA, -1).astype(BF16),
        "w_uv": w_ukv[:, :, QK_NOPE:].reshape(KV_LORA, -1).T.astype(BF16),
    }
    q, k, v = _mla_proj(hl.reshape(bsz, seq, d), hc.reshape(bsz, lc, d), p, _rope_tables(seq, lc))
    a = _attention(q, k, v, tq=ATTN_TQ, n_chunks=ATTN_KEY_CHUNKS, heads=ATTN_HEADS_PER_STEP)
    xl, hl = _oproj(a.reshape(bsz * seq, -1), xl, lat_mod(1, 2), mla_w_o[0].astype(BF16),
                    (norm2_g[1].reshape(1, d), lat_mod(1, 3), lat_mod(1, 4)), tm=OPROJ_TM)
    out = _ffn(hl, xl, lat_mod(1, 5), *ffn_w, (final_norm_g.reshape(1, d),), layer=1,
               tm_up=FFN_UP_TM, tf=FFN_TF, tm_down=FFN_DOWN_TM, seg_len=seq, final=True)
    return out.reshape(bsz, seq, d)
```

```python
import functools

import jax
import jax.numpy as jnp
from jax import lax
from jax.experimental import pallas as pl
from jax.experimental.pallas import tpu as pltpu

F32 = jnp.float32
BF16 = jnp.bfloat16

D_MODEL = 2048
GRID_W = 64
EPS = 1e-6
N_MOD = 6
CHUNK = 128
A_HEADS = 8
A_WIDTH = 1024
B_WIDTH = 1024
B_CONV = 31
MLA_HEADS = 16
Q_LORA = 768
KV_LORA = 512
QK_NOPE = 128
QK_ROPE = 64
V_DIM = 128
ROPE_THETA = 10000.0
D_FF = 5632
LOG2_E = 1.4426950408889634

LANES = 128
BF16_SUBLANES = 16
HEAD_PAD = 256
VMEM_LIMIT = 56 * 2**20

HALO = BF16_SUBLANES
B_PAD = (B_CONV - 1) // 2

NMM_TM, NMM_TN = 1024, 1024
ABMIX_TM = 256
FFN_UP_TM, FFN_TF = 1024, 512
FFN_DOWN_TM = 256
MLA_TM = 256
ATTN_TQ, ATTN_KEY_CHUNKS, ATTN_HEADS_PER_STEP = 1024, 3, 4
OPROJ_TM = 512
CTX_TM = 256


def _cparams(*sem):
    return pltpu.CompilerParams(dimension_semantics=sem, vmem_limit_bytes=VMEM_LIMIT)


def _const_spec(shape):
    n = len(shape)
    return pl.BlockSpec(shape, lambda *_: (0,) * n, pipeline_mode=pl.Buffered(1))


def _rms(x, g):
    return x * lax.rsqrt(jnp.mean(x * x, axis=-1, keepdims=True) + EPS) * g


def _layernorm(x, g, b):
    mu = jnp.mean(x, axis=-1, keepdims=True)
    xc = x - mu
    var = jnp.mean(xc * xc, axis=-1, keepdims=True)
    return xc * lax.rsqrt(var + EPS) * g + b


def _norm_mod(x, g, shift, scale):
    return _rms(x, g) * (1.0 + scale) + shift


def _dot(a, b):
    return jnp.dot(a, b, preferred_element_type=F32)


def _next_specs(d, tiles_per_batch):
    vec = pl.BlockSpec((None, 1, d), lambda i: (i // tiles_per_batch, 0, 0))
    return [pl.BlockSpec((1, d), lambda i: (0, 0)), vec, vec]


def _carry(carry, steps):
    if carry is None:
        return [], [], [], []
    stack, layer = carry
    _, r, c = stack.shape
    rows = r // steps
    assert rows * steps == r and rows % BF16_SUBLANES == 0
    return ([pl.BlockSpec((None, rows, c), lambda i: (layer, i, 0))],
            [pl.BlockSpec((rows, c), lambda i: (i, 0))],
            [jax.ShapeDtypeStruct((r, c), BF16)],
            [stack])


def _carry_cast(src_ref, dst_ref):
    dst_ref[...] = src_ref[...].astype(BF16)


def _adaln_kernel(cv_ref, w_ref, b_ref, o_ref):
    cv = cv_ref[...]
    a = (cv * jax.nn.sigmoid(cv)).astype(BF16)
    o_ref[...] = _dot(a, w_ref[...].astype(BF16)) + b_ref[...]


def _adaln(cv, w_ada, b_ada):
    depth, d, n = w_ada.shape
    tn = 1024
    return pl.pallas_call(
        _adaln_kernel,
        grid=(depth, n // tn),
        in_specs=[
            pl.BlockSpec((8, d), lambda l, j: (0, 0)),
            pl.BlockSpec((None, d, tn), lambda l, j: (l, 0, j)),
            pl.BlockSpec((None, 1, tn), lambda l, j: (l, 0, j)),
        ],
        out_specs=pl.BlockSpec((None, 8, tn), lambda l, j: (l, 0, j)),
        out_shape=jax.ShapeDtypeStruct((depth, 8, n), F32),
        compiler_params=_cparams("parallel", "arbitrary"),
        name="adaln",
    )(cv, w_ada, b_ada.reshape(depth, 1, n))


def _nmm_kernel(x_ref, g_ref, sh_ref, sc_ref, w_ref, b_ref, o_ref, h_ref):
    @pl.when(pl.program_id(1) == 0)
    def _():
        h_ref[...] = _norm_mod(x_ref[...], g_ref[...], sh_ref[...], sc_ref[...]).astype(BF16)

    o_ref[...] = _dot(h_ref[...], w_ref[...]) + b_ref[...]


def _nmm(x, g, shift, scale, w, b, *, tm, tn):
    m, d = x.shape
    n = w.shape[1]
    tiles_per_batch = m // shift.shape[0] // tm
    vec = pl.BlockSpec((None, 1, d), lambda i, j: (i // tiles_per_batch, 0, 0))
    return pl.pallas_call(
        _nmm_kernel,
        grid=(m // tm, n // tn),
        in_specs=[
            pl.BlockSpec((tm, d), lambda i, j: (i, 0)),
            pl.BlockSpec((1, d), lambda i, j: (0, 0)),
            vec, vec,
            pl.BlockSpec((d, tn), lambda i, j: (0, j)),
            pl.BlockSpec((1, tn), lambda i, j: (0, j)),
        ],
        out_specs=pl.BlockSpec((tm, tn), lambda i, j: (i, j)),
        out_shape=jax.ShapeDtypeStruct((m, n), F32),
        scratch_shapes=[pltpu.VMEM((tm, d), BF16)],
        compiler_params=_cparams("parallel", "arbitrary"),
        name="norm_mod_matmul",
    )(x, g, shift, scale, w, b)


_CONV_ROWS = 64


def _abmix_kernel(z_ref, zp_ref, zn_ref, x_ref, gate_ref, alng_ref, alnb_ref, ws_ref, bs_ref,
                  cw_ref, cb_ref, blng_ref, blnb_ref, wout_ref, ng_ref, sh_ref, sc_ref, *rest,
                  tm, tiles_per_seg, carry):
    if carry:
        cast_src_ref, o_ref, hn_ref, cast_dst_ref, e_ref, hc_ref, y_ref = rest
        _carry_cast(cast_src_ref, cast_dst_ref)
    else:
        o_ref, hn_ref, e_ref, hc_ref, y_ref = rest
    i = pl.program_id(0)
    pos = i % tiles_per_seg
    nch = tm // CHUNK

    ga = jax.nn.gelu(z_ref[:, :2 * A_WIDTH], approximate=True)
    u = ga[:, :A_WIDTH]
    vb = _layernorm(ga[:, A_WIDTH:], alng_ref[...], alnb_ref[...]).astype(BF16)
    for h in range(A_HEADS):
        cols = slice(h * CHUNK, (h + 1) * CHUNK)
        rhs = jnp.concatenate([vb[c * CHUNK:(c + 1) * CHUNK, cols] for c in range(nch)], axis=1)
        r = _dot(ws_ref[h], rhs)
        for c in range(nch):
            rows = slice(c * CHUNK, (c + 1) * CHUNK)
            mixed = r[:, c * CHUNK:(c + 1) * CHUNK] + bs_ref[:, cols]
            y_ref[rows, cols] = (u[rows, cols] * mixed).astype(BF16)

    def glu(zz):
        return zz[:, :B_WIDTH] * jax.nn.sigmoid(zz[:, B_WIDTH:])

    def put(r0, val):
        for c in range(B_WIDTH // LANES):
            e_ref[c, r0:r0 + val.shape[0], :] = val[:, c * LANES:(c + 1) * LANES]

    put(0, jnp.where(pos == 0, 0.0, glu(zp_ref[...])))
    put(HALO, glu(z_ref[:, 2 * A_WIDTH:]))
    put(HALO + tm, jnp.where(pos == tiles_per_seg - 1, 0.0, glu(zn_ref[...])))

    base = HALO - B_PAD
    for c in range(B_WIDTH // LANES):
        lanes = slice(c * LANES, (c + 1) * LANES)
        wk = [cw_ref[k:k + 1, lanes] for k in range(B_CONV)]
        bias = cb_ref[:, lanes]

        def row_block(rb, carry, c=c, lanes=lanes, wk=wk, bias=bias):
            r0 = pl.multiple_of(rb * _CONV_ROWS, _CONV_ROWS)
            acc = e_ref[c, pl.ds(r0 + base, _CONV_ROWS), :] * wk[0]
            for k in range(1, B_CONV):
                acc = acc + e_ref[c, pl.ds(r0 + base + k, _CONV_ROWS), :] * wk[k]
            hc_ref[pl.ds(r0, _CONV_ROWS), lanes] = acc + bias
            return carry

        lax.fori_loop(0, tm // _CONV_ROWS, row_block, 0)

    hb = _layernorm(hc_ref[...], blng_ref[...], blnb_ref[...])
    y_ref[:, A_WIDTH:] = (hb * jax.nn.sigmoid(hb)).astype(BF16)

    y = x_ref[...] + gate_ref[...] * _dot(y_ref[...], wout_ref[...])
    o_ref[...] = y
    hn_ref[...] = _norm_mod(y, ng_ref[...], sh_ref[...], sc_ref[...]).astype(BF16)


def _abmix(z, x, gate, p, nxt, *, tm, seg_len, carry=None):
    m, d = x.shape
    tiles_per_seg = seg_len // tm
    tiles_per_batch = m // gate.shape[0] // tm
    hb = tm // HALO
    nhalo = m // HALO
    c_in, c_out, c_shape, c_args = _carry(carry, m // tm)
    kern = functools.partial(_abmix_kernel, tm=tm, tiles_per_seg=tiles_per_seg, carry=carry is not None)
    row = lambda w: pl.BlockSpec((1, w), lambda i: (0, 0))
    tile = pl.BlockSpec((tm, d), lambda i: (i, 0))
    return pl.pallas_call(
        kern,
        grid=(m // tm,),
        in_specs=[
            pl.BlockSpec((tm, 2 * A_WIDTH + 2 * B_WIDTH), lambda i: (i, 0)),
            pl.BlockSpec((HALO, 2 * B_WIDTH), lambda i: (jnp.maximum(i * hb - 1, 0), 1)),
            pl.BlockSpec((HALO, 2 * B_WIDTH), lambda i: (jnp.minimum((i + 1) * hb, nhalo - 1), 1)),
            tile,
            pl.BlockSpec((None, 1, d), lambda i: (i // tiles_per_batch, 0, 0)),
            row(A_WIDTH), row(A_WIDTH),
            _const_spec((A_HEADS, CHUNK, CHUNK)),
            _const_spec((CHUNK, A_WIDTH)),
            _const_spec((B_CONV, B_WIDTH)),
            row(B_WIDTH), row(B_WIDTH), row(B_WIDTH),
            _const_spec((A_WIDTH + B_WIDTH, d)),
            *_next_specs(d, tiles_per_batch),
            *c_in,
        ],
        out_specs=[tile, tile, *c_out],
        out_shape=[jax.ShapeDtypeStruct((m, d), F32), jax.ShapeDtypeStruct((m, d), BF16), *c_shape],
        scratch_shapes=[
            pltpu.VMEM((B_WIDTH // LANES, tm + 2 * HALO, LANES), F32),
            pltpu.VMEM((tm, B_WIDTH), F32),
            pltpu.VMEM((tm, A_WIDTH + B_WIDTH), BF16),
        ],
        compiler_params=_cparams("parallel"),
        name="ab_mixer",
    )(z, z, z, x, gate, p["a_ln_g"], p["a_ln_b"], p["w_s"], p["b_s_full"], p["conv_w"], p["conv_b"],
      p["b_ln_g"], p["b_ln_b"], p["w_out"], *nxt, *c_args)


def _ffn_up_kernel(h_ref, hp_ref, hn_ref, wg_ref, wu_ref, cw_ref, cb_ref, a_ref, hx_ref, g_ref, *, tm, seg_len):
    i = pl.program_id(0)
    f = pl.program_id(1)
    tiles_per_seg = max(seg_len // tm, 1)
    pos = i % tiles_per_seg

    @pl.when(f == 0)
    def _():
        halo_zeros = jnp.zeros(hp_ref.shape, BF16)
        hx_ref[0:HALO, :] = jnp.where(pos == 0, halo_zeros, hp_ref[...])
        hx_ref[HALO:HALO + tm, :] = h_ref[...]
        hx_ref[HALO + tm:, :] = jnp.where(pos == tiles_per_seg - 1, halo_zeros, hn_ref[...])

    g = _dot(hx_ref[...], wg_ref[...].astype(BF16))
    u = _dot(hx_ref[HALO:HALO + tm, :], wu_ref[...].astype(BF16))
    for c in range(g.shape[1] // LANES):
        lanes = slice(c * LANES, (c + 1) * LANES)
        g_ref[c] = g[:, lanes]
        g_prev = g_ref[c, pl.ds(HALO - 1, tm), :]
        g_next = g_ref[c, pl.ds(HALO + 1, tm), :]
        if tm > seg_len:
            r = lax.broadcasted_iota(jnp.int32, g_prev.shape, 0) % seg_len
            g_prev = jnp.where(r == 0, 0.0, g_prev)
            g_next = jnp.where(r == seg_len - 1, 0.0, g_next)
        gc = (cw_ref[0:1, lanes] * g_prev + cw_ref[1:2, lanes] * g[HALO:HALO + tm, lanes]
              + cw_ref[2:3, lanes] * g_next + cb_ref[:, lanes])
        a_ref[:, lanes] = (gc * jax.nn.sigmoid(gc) * u[:, lanes]).astype(BF16)


def _ffn_down_kernel(a_ref, x_ref, gate_ref, wd_ref, ng_ref, *rest, final):
    y = x_ref[...] + gate_ref[...] * _dot(a_ref[...], wd_ref[...])
    if final:
        (o_ref,) = rest
        o_ref[...] = _rms(y, ng_ref[...])
    else:
        sh_ref, sc_ref, o_ref, hn_ref = rest
        o_ref[...] = y
        hn_ref[...] = _norm_mod(y, ng_ref[...], sh_ref[...], sc_ref[...]).astype(BF16)


def _ffn(h, x, gate, w_up, conv_w, conv_b, w_down, nxt, *, layer, tm_up, tf, tm_down, seg_len, final):
    m, d = x.shape
    nf = D_FF // tf
    hb = tm_up // HALO
    nhalo = m // HALO
    assert seg_len % tm_up == 0 or tm_up % seg_len == 0
    kern = functools.partial(_ffn_up_kernel, tm=tm_up, seg_len=seg_len)
    a = pl.pallas_call(
        kern,
        grid=(m // tm_up, nf),
        in_specs=[
            pl.BlockSpec((tm_up, d), lambda i, f: (i, 0)),
            pl.BlockSpec((HALO, d), lambda i, f: (jnp.maximum(i * hb - 1, 0), 0)),
            pl.BlockSpec((HALO, d), lambda i, f: (jnp.minimum((i + 1) * hb, nhalo - 1), 0)),
            pl.BlockSpec((None, d, tf), lambda i, f: (layer, 0, f)),
            pl.BlockSpec((None, d, tf), lambda i, f: (layer, 0, nf + f)),
            pl.BlockSpec((None, 3, tf), lambda i, f: (layer, 0, f)),
            pl.BlockSpec((None, 1, tf), lambda i, f: (layer, 0, f)),
        ],
        out_specs=pl.BlockSpec((tm_up, tf), lambda i, f: (i, f)),
        out_shape=jax.ShapeDtypeStruct((m, D_FF), BF16),
        scratch_shapes=[
            pltpu.VMEM((tm_up + 2 * HALO, d), BF16),
            pltpu.VMEM((tf // LANES, tm_up + 2 * HALO, LANES), F32),
        ],
        compiler_params=_cparams("parallel", "arbitrary"),
        name="ffn_up_conv_gate",
    )(h, h, h, w_up, w_up, conv_w, conv_b)

    tiles_per_batch = m // gate.shape[0] // tm_down
    tile = pl.BlockSpec((tm_down, d), lambda i: (i, 0))
    return pl.pallas_call(
        functools.partial(_ffn_down_kernel, final=final),
        grid=(m // tm_down,),
        in_specs=[
            pl.BlockSpec((tm_down, D_FF), lambda i: (i, 0)),
            tile,
            pl.BlockSpec((None, 1, d), lambda i: (i // tiles_per_batch, 0, 0)),
            _const_spec((D_FF, d)),
            *_next_specs(d, tiles_per_batch)[:1 if final else 3],
        ],
        out_specs=tile if final else [tile, tile],
        out_shape=(jax.ShapeDtypeStruct((m, d), F32) if final else
                   [jax.ShapeDtypeStruct((m, d), F32), jax.ShapeDtypeStruct((m, d), BF16)]),
        compiler_params=_cparams("parallel"),
        name="ffn_down",
    )(a, x, gate, w_down, *nxt)


_ZPAD = Q_LORA + KV_LORA + LANES


def _rope_tile(xt, c, s1, s2, odd):
    if odd:
        return pltpu.roll(xt, 64, 1) * c + pltpu.roll(xt, 32, 1) * s1 + pltpu.roll(xt, 96, 1) * s2
    return xt * c + pltpu.roll(xt, 96, 1) * s1 + pltpu.roll(xt, 32, 1) * s2


def _mla_proj_kernel(hl_ref, hc_ref, win_ref, qg_ref, kvg_ref, wqn_ref, wqp_ref,
                     wuk_ref, wuv_ref, c_ref, s1_ref, s2_ref, q_ref, k_ref, v_ref, *, n_lat_tiles, q_scale):
    t = pl.program_id(1)
    is_ctx = t >= n_lat_tiles
    h = jnp.where(is_ctx, hc_ref[...], hl_ref[...])
    z = _dot(h, win_ref[...])
    c, s1, s2 = c_ref[...], s1_ref[...], s2_ref[...]

    ckvn = _rms(z[:, Q_LORA:Q_LORA + KV_LORA], kvg_ref[...]).astype(BF16)
    kpe = _rope_tile(z[:, Q_LORA + KV_LORA:], c, s1, s2, False).astype(BF16)
    kn = _dot(ckvn, wuk_ref[...])
    v_ref[...] = lax.dot_general(wuv_ref[...], ckvn, (((1,), (1,)), ((), ())),
                                 preferred_element_type=F32).astype(BF16)
    for hd in range(MLA_HEADS):
        k_ref[:, hd * HEAD_PAD:hd * HEAD_PAD + QK_NOPE] = kn[:, hd * QK_NOPE:(hd + 1) * QK_NOPE].astype(BF16)
        k_ref[:, hd * HEAD_PAD + QK_NOPE:(hd + 1) * HEAD_PAD] = kpe

    @pl.when(jnp.logical_not(is_ctx))
    def _():
        cqn = _rms(z[:, :Q_LORA], qg_ref[...]).astype(BF16)
        qn = _dot(cqn, wqn_ref[...])
        qp = _dot(cqn, wqp_ref[...])
        for hd in range(MLA_HEADS):
            q_ref[:, hd * HEAD_PAD:hd * HEAD_PAD + QK_NOPE] = (
                qn[:, hd * QK_NOPE:(hd + 1) * QK_NOPE] * q_scale).astype(BF16)
            xt = qp[:, (hd // 2) * LANES:(hd // 2 + 1) * LANES]
            q_ref[:, hd * HEAD_PAD + QK_NOPE:(hd + 1) * HEAD_PAD] = (
                _rope_tile(xt, c, s1, s2, hd % 2 == 1) * q_scale).astype(BF16)


def _mla_proj(hl, hc, p, tabs):
    bsz, seq, d = hl.shape
    lc = hc.shape[1]
    tm = MLA_TM
    assert lc == tm
    nlt = seq // tm
    nt = nlt + 1
    kern = functools.partial(_mla_proj_kernel, n_lat_tiles=nlt,
                             q_scale=float((QK_NOPE + QK_ROPE) ** -0.5 * LOG2_E))
    tab = pl.BlockSpec((tm, LANES), lambda b, t: (t, 0))
    hw = MLA_HEADS * HEAD_PAD
    return pl.pallas_call(
        kern,
        grid=(bsz, nt),
        in_specs=[
            pl.BlockSpec((None, tm, d), lambda b, t: (b, jnp.minimum(t, nlt - 1), 0)),
            pl.BlockSpec((None, tm, d), lambda b, t: (b, 0, 0)),
            _const_spec((d, _ZPAD)),
            pl.BlockSpec((1, Q_LORA), lambda b, t: (0, 0)),
            pl.BlockSpec((1, KV_LORA), lambda b, t: (0, 0)),
            _const_spec((Q_LORA, MLA_HEADS * QK_NOPE)),
            _const_spec((Q_LORA, MLA_HEADS * QK_ROPE)),
            _const_spec((KV_LORA, MLA_HEADS * QK_NOPE)),
            _const_spec((MLA_HEADS * V_DIM, KV_LORA)),
            tab, tab, tab,
        ],
        out_specs=[
            pl.BlockSpec((None, tm, hw), lambda b, t: (b, jnp.minimum(t, nlt - 1), 0)),
            pl.BlockSpec((None, tm, hw), lambda b, t: (b, t, 0)),
            pl.BlockSpec((None, MLA_HEADS * V_DIM, tm), lambda b, t: (b, 0, t)),
        ],
        out_shape=[
            jax.ShapeDtypeStruct((bsz, seq, hw), BF16),
            jax.ShapeDtypeStruct((bsz, seq + lc, hw), BF16),
            jax.ShapeDtypeStruct((bsz, MLA_HEADS * V_DIM, seq + lc), BF16),
        ],
        compiler_params=_cparams("parallel", "arbitrary"),
        name="mla_proj",
    )(hl, hc, p["w_in"], p["q_norm_g"], p["kv_norm_g"], p["w_uq_n"], p["w_uq_p"],
      p["w_uk"], p["w_uv"], *tabs)


def _key_chunks(lk, n):
    tiles = lk // HEAD_PAD
    assert tiles * HEAD_PAD == lk and tiles >= n
    sizes = [(tiles // n + (1 if c >= n - tiles % n else 0)) * HEAD_PAD for c in range(n)]
    starts = [sum(sizes[:c]) for c in range(n)]
    return list(zip(starts, sizes))


def _attn_kernel(q_ref, k_ref, vt_ref, o_ref, *, chunks, heads):
    items = [(j, c) for j in range(heads) for c in range(len(chunks))]

    def scores(j, c):
        start, size = chunks[c]
        return lax.dot_general(k_ref[start:start + size, j * HEAD_PAD:(j + 1) * HEAD_PAD],
                               q_ref[:, j * HEAD_PAD:(j + 1) * HEAD_PAD], (((1,), (1,)), ((), ())),
                               preferred_element_type=F32)

    m = l = acc = None
    s_next = scores(*items[0])
    for n, (j, c) in enumerate(items):
        start, size = chunks[c]
        s = s_next
        if n + 1 < len(items):
            s_next = scores(*items[n + 1])
        mc = jnp.max(s, axis=0, keepdims=True)
        m_new = mc if c == 0 else jnp.maximum(m, mc)
        p = jnp.exp2(s - m_new)
        lc = jnp.sum(p, axis=0, keepdims=True)
        pv = _dot(vt_ref[j * V_DIM:(j + 1) * V_DIM, start:start + size], p.astype(BF16))
        if c == 0:
            l, acc = lc, pv
        else:
            alpha = jnp.exp2(m - m_new)
            l = alpha * l + lc
            acc = alpha * acc + pv
        m = m_new
        if c == len(chunks) - 1:
            o_ref[:, j * V_DIM:(j + 1) * V_DIM] = (acc / l).T.astype(o_ref.dtype)


def _attention(q, k, v, *, tq, n_chunks, heads):
    bsz, lq, _ = q.shape
    lk = k.shape[1]
    return pl.pallas_call(
        functools.partial(_attn_kernel, chunks=_key_chunks(lk, n_chunks), heads=heads),
        grid=(bsz, MLA_HEADS // heads, lq // tq),
        in_specs=[
            pl.BlockSpec((None, tq, heads * HEAD_PAD), lambda b, h, i: (b, i, h)),
            pl.BlockSpec((None, lk, heads * HEAD_PAD), lambda b, h, i: (b, 0, h)),
            pl.BlockSpec((None, heads * V_DIM, lk), lambda b, h, i: (b, h, 0)),
        ],
        out_specs=pl.BlockSpec((None, tq, heads * V_DIM), lambda b, h, i: (b, i, h)),
        out_shape=jax.ShapeDtypeStruct((bsz, lq, MLA_HEADS * V_DIM), BF16),
        compiler_params=_cparams("parallel", "parallel", "arbitrary"),
        name="mla_attention",
    )(q, k, v)


def _oproj_kernel(a_ref, x_ref, gate_ref, w_ref, ng_ref, sh_ref, sc_ref, *rest, carry):
    if carry:
        cast_src_ref, o_ref, hn_ref, cast_dst_ref = rest
        _carry_cast(cast_src_ref, cast_dst_ref)
    else:
        o_ref, hn_ref = rest
    y = x_ref[...] + gate_ref[...] * _dot(a_ref[...], w_ref[...])
    o_ref[...] = y
    hn_ref[...] = _norm_mod(y, ng_ref[...], sh_ref[...], sc_ref[...]).astype(BF16)


def _oproj(a, x, gate, w, nxt, *, tm, carry=None):
    m, d = x.shape
    kdim = a.shape[1]
    tiles_per_batch = m // gate.shape[0] // tm
    tile = pl.BlockSpec((tm, d), lambda i: (i, 0))
    c_in, c_out, c_shape, c_args = _carry(carry, m // tm)
    return pl.pallas_call(
        functools.partial(_oproj_kernel, carry=carry is not None),
        grid=(m // tm,),
        in_specs=[
            pl.BlockSpec((tm, kdim), lambda i: (i, 0)),
            tile,
            pl.BlockSpec((None, 1, d), lambda i: (i // tiles_per_batch, 0, 0)),
            _const_spec((kdim, d)),
            *_next_specs(d, tiles_per_batch),
            *c_in,
        ],
        out_specs=[tile, tile, *c_out],
        out_shape=[jax.ShapeDtypeStruct((m, d), F32), jax.ShapeDtypeStruct((m, d), BF16), *c_shape],
        compiler_params=_cparams("parallel"),
        name="attn_out_proj",
    )(a, x, gate, w, *nxt, *c_args)


def _rope_tables(seq, ctx_len):
    rows = seq // GRID_W
    row = jnp.repeat(jnp.arange(rows, dtype=F32), GRID_W)
    col = jnp.tile(jnp.arange(GRID_W, dtype=F32), rows)
    n_freq = QK_ROPE // 4
    inv = ROPE_THETA ** (-jnp.arange(n_freq, dtype=F32) / n_freq)
    ang = jnp.concatenate([row[:, None] * inv, col[:, None] * inv], axis=-1)
    cos, sin = jnp.cos(ang), jnp.sin(ang)
    half = QK_ROPE // 2
    cos = jnp.concatenate([cos, jnp.ones((ctx_len, half), F32)], axis=0)
    sin = jnp.concatenate([sin, jnp.zeros((ctx_len, half), F32)], axis=0)
    zh = jnp.zeros_like(cos)
    c = jnp.concatenate([cos, cos, zh, zh], axis=-1)
    s1 = jnp.concatenate([-sin, zh, zh, zh], axis=-1)
    s2 = jnp.concatenate([zh, sin, zh, zh], axis=-1)
    return c, s1, s2


def kernel(x, c, ctx, c_ctx, norm1_g, norm2_g, w_ada, b_ada, ab_w_in, ab_b_in, a_ln_g, a_ln_b, a_w_s, a_b_s, b_conv_w, b_conv_b, b_ln_g, b_ln_b, ab_w_out, mla_w_in, mla_q_norm_g, mla_w_uq, mla_kv_norm_g, mla_w_ukv, mla_w_o, ffn_w_up, ffn_conv_w, ffn_conv_b, ffn_w_down, final_norm_g):
    bsz, seq, d = x.shape
    lc = ctx.shape[1]
    depth = w_ada.shape[0]
    assert depth == 2, "the call chain below is written for one even (mixer) layer followed by one MLA layer"

    cv = jnp.concatenate([c, c_ctx[None, :], jnp.zeros((8 - bsz - 1, d), F32)], axis=0)
    mods = _adaln(cv, w_ada, b_ada)

    def lat_mod(i, k):
        return mods[i, :bsz, k * d:(k + 1) * d].reshape(bsz, 1, d)

    def ctx_mod(i, k):
        return mods[i, bsz:bsz + 1, k * d:(k + 1) * d].reshape(1, 1, d)

    ffn_w = (ffn_w_up, ffn_conv_w, ffn_conv_b.reshape(depth, 1, -1))
    xl = x.reshape(bsz * seq, d)
    xc = ctx.reshape(bsz * lc, d)
    n_ctx_rows = bsz * lc

    p = {
        "a_ln_g": a_ln_g[0].reshape(1, -1), "a_ln_b": a_ln_b[0].reshape(1, -1),
        "w_s": a_w_s[0].astype(BF16),
        "b_s_full": jnp.repeat(a_b_s[0].T, A_WIDTH // A_HEADS, axis=1),
        "conv_w": b_conv_w[0], "conv_b": b_conv_b[0].reshape(1, -1),
        "b_ln_g": b_ln_g[0].reshape(1, -1), "b_ln_b": b_ln_b[0].reshape(1, -1),
        "w_out": ab_w_out[0].astype(BF16),
    }
    w_in = ab_w_in[0].astype(BF16)
    b_in = ab_b_in[0].reshape(1, -1)
    ng1, ng2 = norm1_g[0].reshape(1, d), norm2_g[0].reshape(1, d)
    ng1_next = norm1_g[1].reshape(1, d)

    z = _nmm(xl, ng1, lat_mod(0, 0), lat_mod(0, 1), w_in, b_in, tm=NMM_TM, tn=NMM_TN)
    xl, hl, w_down0 = _abmix(z, xl, lat_mod(0, 2), p, (ng2, lat_mod(0, 3), lat_mod(0, 4)), tm=ABMIX_TM, seg_len=seq,
                             carry=(ffn_w_down, 0))
    xl, hl = _ffn(hl, xl, lat_mod(0, 5), *ffn_w, w_down0, (ng1_next, lat_mod(1, 0), lat_mod(1, 1)), layer=0,
                  tm_up=FFN_UP_TM, tf=FFN_TF, tm_down=FFN_DOWN_TM, seg_len=seq, final=False)

    zc = _nmm(xc, ng1, ctx_mod(0, 0), ctx_mod(0, 1), w_in, b_in, tm=n_ctx_rows, tn=NMM_TN)
    xc, hc = _abmix(zc, xc, ctx_mod(0, 2), p, (ng2, ctx_mod(0, 3), ctx_mod(0, 4)), tm=CTX_TM, seg_len=lc)
    xc, hc = _ffn(hc, xc, ctx_mod(0, 5), *ffn_w, w_down0, (ng1_next, ctx_mod(1, 0), ctx_mod(1, 1)), layer=0,
                  tm_up=n_ctx_rows, tf=FFN_TF, tm_down=CTX_TM, seg_len=lc, final=False)

    w_uq = mla_w_uq[0].reshape(Q_LORA, MLA_HEADS, QK_NOPE + QK_ROPE)
    w_ukv = mla_w_ukv[0].reshape(KV_LORA, MLA_HEADS, QK_NOPE + V_DIM)
    p = {
        "w_in": jnp.pad(mla_w_in[0], ((0, 0), (0, LANES - QK_ROPE))).astype(BF16),
        "q_norm_g": mla_q_norm_g[0].reshape(1, -1), "kv_norm_g": mla_kv_norm_g[0].reshape(1, -1),
        "w_uq_n": w_uq[:, :, :QK_NOPE].reshape(Q_LORA, -1).astype(BF16),
        "w_uq_p": w_uq[:, :, QK_NOPE:].reshape(Q_LORA, -1).astype(BF16),
        "w_uk": w_ukv[:, :, :QK_NOPE].reshape(KV_LORA, -1).astype(BF16),
        "w_uv": w_ukv[:, :, QK_NOPE:].reshape(KV_LORA, -1).T.astype(BF16),
    }
    q, k, v = _mla_proj(hl.reshape(bsz, seq, d), hc.reshape(bsz, lc, d), p, _rope_tables(seq, lc))
    a = _attention(q, k, v, tq=ATTN_TQ, n_chunks=ATTN_KEY_CHUNKS, heads=ATTN_HEADS_PER_STEP)
    xl, hl, w_down1 = _oproj(a.reshape(bsz * seq, -1), xl, lat_mod(1, 2), mla_w_o[0].astype(BF16),
                             (norm2_g[1].reshape(1, d), lat_mod(1, 3), lat_mod(1, 4)), tm=OPROJ_TM,
                             carry=(ffn_w_down, 1))
    out = _ffn(hl, xl, lat_mod(1, 5), *ffn_w, w_down1, (final_norm_g.reshape(1, d),), layer=1,
               tm_up=FFN_UP_TM, tf=FFN_TF, tm_down=FFN_DOWN_TM, seg_len=seq, final=True)
    return out.reshape(bsz, seq, d)
```

```python
import functools

import jax
import jax.numpy as jnp
from jax import lax
from jax.experimental import pallas as pl
from jax.experimental.pallas import tpu as pltpu

F32 = jnp.float32
BF16 = jnp.bfloat16

D_MODEL = 2048
GRID_W = 64
EPS = 1e-6
N_MOD = 6
CHUNK = 128
A_HEADS = 8
A_WIDTH = 1024
B_WIDTH = 1024
B_CONV = 31
MLA_HEADS = 16
Q_LORA = 768
KV_LORA = 512
QK_NOPE = 128
QK_ROPE = 64
V_DIM = 128
ROPE_THETA = 10000.0
D_FF = 5632
LOG2_E = 1.4426950408889634

LANES = 128
BF16_SUBLANES = 16
HEAD_PAD = 256
VMEM_LIMIT = 56 * 2**20

HALO = BF16_SUBLANES
B_PAD = (B_CONV - 1) // 2

NMM_TM, NMM_TN = 1024, 1024
ABMIX_TM = 256
FFN_UP_TM, FFN_TF = 1024, 512
FFN_DOWN_TM = 256
MLA_TM = 256
ATTN_TQ, ATTN_KEY_CHUNKS, ATTN_HEADS_PER_STEP = 512, 3, 4
OPROJ_TM = 512
CTX_TM = 256


def _cparams(*sem):
    return pltpu.CompilerParams(dimension_semantics=sem, vmem_limit_bytes=VMEM_LIMIT)


def _const_spec(shape):
    n = len(shape)
    return pl.BlockSpec(shape, lambda *_: (0,) * n, pipeline_mode=pl.Buffered(1))


def _rms(x, g):
    return x * lax.rsqrt(jnp.mean(x * x, axis=-1, keepdims=True) + EPS) * g


def _layernorm(x, g, b):
    mu = jnp.mean(x, axis=-1, keepdims=True)
    xc = x - mu
    var = jnp.mean(xc * xc, axis=-1, keepdims=True)
    return xc * lax.rsqrt(var + EPS) * g + b


def _norm_mod(x, g, shift, scale):
    return _rms(x, g) * (1.0 + scale) + shift


def _dot(a, b):
    return jnp.dot(a, b, preferred_element_type=F32)


def _next_specs(d, tiles_per_batch):
    vec = pl.BlockSpec((None, 1, d), lambda i: (i // tiles_per_batch, 0, 0))
    return [pl.BlockSpec((1, d), lambda i: (0, 0)), vec, vec]


def _carry(carry, steps):
    if carry is None:
        return [], [], [], []
    stack, layer = carry
    _, r, c = stack.shape
    rows = r // steps
    assert rows * steps == r and rows % BF16_SUBLANES == 0
    return ([pl.BlockSpec((None, rows, c), lambda i: (layer, i, 0))],
            [pl.BlockSpec((rows, c), lambda i: (i, 0))],
            [jax.ShapeDtypeStruct((r, c), BF16)],
            [stack])


def _carry_cast(src_ref, dst_ref):
    dst_ref[...] = src_ref[...].astype(BF16)


def _adaln_kernel(cv_ref, w_ref, b_ref, o_ref):
    cv = cv_ref[...]
    a = (cv * jax.nn.sigmoid(cv)).astype(BF16)
    o_ref[...] = _dot(a, w_ref[...].astype(BF16)) + b_ref[...]


def _adaln(cv, w_ada, b_ada):
    depth, d, n = w_ada.shape
    tn = 1024
    return pl.pallas_call(
        _adaln_kernel,
        grid=(depth, n // tn),
        in_specs=[
            pl.BlockSpec((8, d), lambda l, j: (0, 0)),
            pl.BlockSpec((None, d, tn), lambda l, j: (l, 0, j)),
            pl.BlockSpec((None, 1, tn), lambda l, j: (l, 0, j)),
        ],
        out_specs=pl.BlockSpec((None, 8, tn), lambda l, j: (l, 0, j)),
        out_shape=jax.ShapeDtypeStruct((depth, 8, n), F32),
        compiler_params=_cparams("parallel", "arbitrary"),
        name="adaln",
    )(cv, w_ada, b_ada.reshape(depth, 1, n))


def _nmm_kernel(x_ref, g_ref, sh_ref, sc_ref, w_ref, b_ref, o_ref, h_ref):
    @pl.when(pl.program_id(1) == 0)
    def _():
        h_ref[...] = _norm_mod(x_ref[...], g_ref[...], sh_ref[...], sc_ref[...]).astype(BF16)

    o_ref[...] = _dot(h_ref[...], w_ref[...]) + b_ref[...]


def _nmm(x, g, shift, scale, w, b, *, tm, tn):
    m, d = x.shape
    n = w.shape[1]
    tiles_per_batch = m // shift.shape[0] // tm
    vec = pl.BlockSpec((None, 1, d), lambda i, j: (i // tiles_per_batch, 0, 0))
    return pl.pallas_call(
        _nmm_kernel,
        grid=(m // tm, n // tn),
        in_specs=[
            pl.BlockSpec((tm, d), lambda i, j: (i, 0)),
            pl.BlockSpec((1, d), lambda i, j: (0, 0)),
            vec, vec,
            pl.BlockSpec((d, tn), lambda i, j: (0, j)),
            pl.BlockSpec((1, tn), lambda i, j: (0, j)),
        ],
        out_specs=pl.BlockSpec((tm, tn), lambda i, j: (i, j)),
        out_shape=jax.ShapeDtypeStruct((m, n), F32),
        scratch_shapes=[pltpu.VMEM((tm, d), BF16)],
        compiler_params=_cparams("parallel", "arbitrary"),
        name="norm_mod_matmul",
    )(x, g, shift, scale, w, b)


_CONV_ROWS = 64


def _abmix_kernel(z_ref, zp_ref, zn_ref, x_ref, gate_ref, alng_ref, alnb_ref, ws_ref, bs_ref,
                  cw_ref, cb_ref, blng_ref, blnb_ref, wout_ref, ng_ref, sh_ref, sc_ref, *rest,
                  tm, tiles_per_seg, carry):
    if carry:
        cast_src_ref, o_ref, hn_ref, cast_dst_ref, e_ref, hc_ref, y_ref = rest
        _carry_cast(cast_src_ref, cast_dst_ref)
    else:
        o_ref, hn_ref, e_ref, hc_ref, y_ref = rest
    i = pl.program_id(0)
    pos = i % tiles_per_seg
    nch = tm // CHUNK

    ga = jax.nn.gelu(z_ref[:, :2 * A_WIDTH], approximate=True)
    u = ga[:, :A_WIDTH]
    vb = _layernorm(ga[:, A_WIDTH:], alng_ref[...], alnb_ref[...]).astype(BF16)
    for h in range(A_HEADS):
        cols = slice(h * CHUNK, (h + 1) * CHUNK)
        rhs = jnp.concatenate([vb[c * CHUNK:(c + 1) * CHUNK, cols] for c in range(nch)], axis=1)
        r = _dot(ws_ref[h], rhs)
        for c in range(nch):
            rows = slice(c * CHUNK, (c + 1) * CHUNK)
            mixed = r[:, c * CHUNK:(c + 1) * CHUNK] + bs_ref[:, cols]
            y_ref[rows, cols] = (u[rows, cols] * mixed).astype(BF16)

    def glu(zz):
        return zz[:, :B_WIDTH] * jax.nn.sigmoid(zz[:, B_WIDTH:])

    def put(r0, val):
        for c in range(B_WIDTH // LANES):
            e_ref[c, r0:r0 + val.shape[0], :] = val[:, c * LANES:(c + 1) * LANES]

    put(0, jnp.where(pos == 0, 0.0, glu(zp_ref[...])))
    put(HALO, glu(z_ref[:, 2 * A_WIDTH:]))
    put(HALO + tm, jnp.where(pos == tiles_per_seg - 1, 0.0, glu(zn_ref[...])))

    base = HALO - B_PAD
    for c in range(B_WIDTH // LANES):
        lanes = slice(c * LANES, (c + 1) * LANES)
        wk = [cw_ref[k:k + 1, lanes] for k in range(B_CONV)]
        bias = cb_ref[:, lanes]

        def row_block(rb, carry, c=c, lanes=lanes, wk=wk, bias=bias):
            r0 = pl.multiple_of(rb * _CONV_ROWS, _CONV_ROWS)
            acc = e_ref[c, pl.ds(r0 + base, _CONV_ROWS), :] * wk[0]
            for k in range(1, B_CONV):
                acc = acc + e_ref[c, pl.ds(r0 + base + k, _CONV_ROWS), :] * wk[k]
            hc_ref[pl.ds(r0, _CONV_ROWS), lanes] = acc + bias
            return carry

        lax.fori_loop(0, tm // _CONV_ROWS, row_block, 0)

    hb = _layernorm(hc_ref[...], blng_ref[...], blnb_ref[...])
    y_ref[:, A_WIDTH:] = (hb * jax.nn.sigmoid(hb)).astype(BF16)

    y = x_ref[...] + gate_ref[...] * _dot(y_ref[...], wout_ref[...])
    o_ref[...] = y
    hn_ref[...] = _norm_mod(y, ng_ref[...], sh_ref[...], sc_ref[...]).astype(BF16)


def _abmix(z, x, gate, p, nxt, *, tm, seg_len, carry=None):
    m, d = x.shape
    tiles_per_seg = seg_len // tm
    tiles_per_batch = m // gate.shape[0] // tm
    hb = tm // HALO
    nhalo = m // HALO
    c_in, c_out, c_shape, c_args = _carry(carry, m // tm)
    kern = functools.partial(_abmix_kernel, tm=tm, tiles_per_seg=tiles_per_seg, carry=carry is not None)
    row = lambda w: pl.BlockSpec((1, w), lambda i: (0, 0))
    tile = pl.BlockSpec((tm, d), lambda i: (i, 0))
    return pl.pallas_call(
        kern,
        grid=(m // tm,),
        in_specs=[
            pl.BlockSpec((tm, 2 * A_WIDTH + 2 * B_WIDTH), lambda i: (i, 0)),
            pl.BlockSpec((HALO, 2 * B_WIDTH), lambda i: (jnp.maximum(i * hb - 1, 0), 1)),
            pl.BlockSpec((HALO, 2 * B_WIDTH), lambda i: (jnp.minimum((i + 1) * hb, nhalo - 1), 1)),
            tile,
            pl.BlockSpec((None, 1, d), lambda i: (i // tiles_per_batch, 0, 0)),
            row(A_WIDTH), row(A_WIDTH),
            _const_spec((A_HEADS, CHUNK, CHUNK)),
            _const_spec((CHUNK, A_WIDTH)),
            _const_spec((B_CONV, B_WIDTH)),
            row(B_WIDTH), row(B_WIDTH), row(B_WIDTH),
            _const_spec((A_WIDTH + B_WIDTH, d)),
            *_next_specs(d, tiles_per_batch),
            *c_in,
        ],
        out_specs=[tile, tile, *c_out],
        out_shape=[jax.ShapeDtypeStruct((m, d), F32), jax.ShapeDtypeStruct((m, d), BF16), *c_shape],
        scratch_shapes=[
            pltpu.VMEM((B_WIDTH // LANES, tm + 2 * HALO, LANES), F32),
            pltpu.VMEM((tm, B_WIDTH), F32),
            pltpu.VMEM((tm, A_WIDTH + B_WIDTH), BF16),
        ],
        compiler_params=_cparams("parallel"),
        name="ab_mixer",
    )(z, z, z, x, gate, p["a_ln_g"], p["a_ln_b"], p["w_s"], p["b_s_full"], p["conv_w"], p["conv_b"],
      p["b_ln_g"], p["b_ln_b"], p["w_out"], *nxt, *c_args)


def _ffn_up_kernel(h_ref, hp_ref, hn_ref, wg_ref, wu_ref, cw_ref, cb_ref, a_ref, hx_ref, g_ref, *, tm, seg_len):
    i = pl.program_id(0)
    f = pl.program_id(1)
    tiles_per_seg = max(seg_len // tm, 1)
    pos = i % tiles_per_seg

    @pl.when(f == 0)
    def _():
        halo_zeros = jnp.zeros(hp_ref.shape, BF16)
        hx_ref[0:HALO, :] = jnp.where(pos == 0, halo_zeros, hp_ref[...])
        hx_ref[HALO:HALO + tm, :] = h_ref[...]
        hx_ref[HALO + tm:, :] = jnp.where(pos == tiles_per_seg - 1, halo_zeros, hn_ref[...])

    g = _dot(hx_ref[...], wg_ref[...].astype(BF16))
    u = _dot(hx_ref[HALO:HALO + tm, :], wu_ref[...].astype(BF16))
    for c in range(g.shape[1] // LANES):
        lanes = slice(c * LANES, (c + 1) * LANES)
        g_ref[c] = g[:, lanes]
        g_prev = g_ref[c, pl.ds(HALO - 1, tm), :]
        g_next = g_ref[c, pl.ds(HALO + 1, tm), :]
        if tm > seg_len:
            r = lax.broadcasted_iota(jnp.int32, g_prev.shape, 0) % seg_len
            g_prev = jnp.where(r == 0, 0.0, g_prev)
            g_next = jnp.where(r == seg_len - 1, 0.0, g_next)
        gc = (cw_ref[0:1, lanes] * g_prev + cw_ref[1:2, lanes] * g[HALO:HALO + tm, lanes]
              + cw_ref[2:3, lanes] * g_next + cb_ref[:, lanes])
        a_ref[:, lanes] = (gc * jax.nn.sigmoid(gc) * u[:, lanes]).astype(BF16)


def _ffn_down_kernel(a_ref, x_ref, gate_ref, wd_ref, ng_ref, *rest, final):
    y = x_ref[...] + gate_ref[...] * _dot(a_ref[...], wd_ref[...])
    if final:
        (o_ref,) = rest
        o_ref[...] = _rms(y, ng_ref[...])
    else:
        sh_ref, sc_ref, o_ref, hn_ref = rest
        o_ref[...] = y
        hn_ref[...] = _norm_mod(y, ng_ref[...], sh_ref[...], sc_ref[...]).astype(BF16)


def _ffn(h, x, gate, w_up, conv_w, conv_b, w_down, nxt, *, layer, tm_up, tf, tm_down, seg_len, final):
    m, d = x.shape
    nf = D_FF // tf
    hb = tm_up // HALO
    nhalo = m // HALO
    assert seg_len % tm_up == 0 or tm_up % seg_len == 0
    kern = functools.partial(_ffn_up_kernel, tm=tm_up, seg_len=seg_len)
    a = pl.pallas_call(
        kern,
        grid=(m // tm_up, nf),
        in_specs=[
            pl.BlockSpec((tm_up, d), lambda i, f: (i, 0)),
            pl.BlockSpec((HALO, d), lambda i, f: (jnp.maximum(i * hb - 1, 0), 0)),
            pl.BlockSpec((HALO, d), lambda i, f: (jnp.minimum((i + 1) * hb, nhalo - 1), 0)),
            pl.BlockSpec((None, d, tf), lambda i, f: (layer, 0, f)),
            pl.BlockSpec((None, d, tf), lambda i, f: (layer, 0, nf + f)),
            pl.BlockSpec((None, 3, tf), lambda i, f: (layer, 0, f)),
            pl.BlockSpec((None, 1, tf), lambda i, f: (layer, 0, f)),
        ],
        out_specs=pl.BlockSpec((tm_up, tf), lambda i, f: (i, f)),
        out_shape=jax.ShapeDtypeStruct((m, D_FF), BF16),
        scratch_shapes=[
            pltpu.VMEM((tm_up + 2 * HALO, d), BF16),
            pltpu.VMEM((tf // LANES, tm_up + 2 * HALO, LANES), F32),
        ],
        compiler_params=_cparams("parallel", "arbitrary"),
        name="ffn_up_conv_gate",
    )(h, h, h, w_up, w_up, conv_w, conv_b)

    tiles_per_batch = m // gate.shape[0] // tm_down
    tile = pl.BlockSpec((tm_down, d), lambda i: (i, 0))
    return pl.pallas_call(
        functools.partial(_ffn_down_kernel, final=final),
        grid=(m // tm_down,),
        in_specs=[
            pl.BlockSpec((tm_down, D_FF), lambda i: (i, 0)),
            tile,
            pl.BlockSpec((None, 1, d), lambda i: (i // tiles_per_batch, 0, 0)),
            _const_spec((D_FF, d)),
            *_next_specs(d, tiles_per_batch)[:1 if final else 3],
        ],
        out_specs=tile if final else [tile, tile],
        out_shape=(jax.ShapeDtypeStruct((m, d), F32) if final else
                   [jax.ShapeDtypeStruct((m, d), F32), jax.ShapeDtypeStruct((m, d), BF16)]),
        compiler_params=_cparams("parallel"),
        name="ffn_down",
    )(a, x, gate, w_down, *nxt)


_ZPAD = Q_LORA + KV_LORA + LANES


def _rope_tile(xt, c, s1, s2, odd):
    if odd:
        return pltpu.roll(xt, 64, 1) * c + pltpu.roll(xt, 32, 1) * s1 + pltpu.roll(xt, 96, 1) * s2
    return xt * c + pltpu.roll(xt, 96, 1) * s1 + pltpu.roll(xt, 32, 1) * s2


def _mla_proj_kernel(hl_ref, hc_ref, win_ref, qg_ref, kvg_ref, wqn_ref, wqp_ref,
                     wuk_ref, wuv_ref, c_ref, s1_ref, s2_ref, q_ref, k_ref, v_ref, *, n_lat_tiles, q_scale):
    t = pl.program_id(1)
    is_ctx = t >= n_lat_tiles
    h = jnp.where(is_ctx, hc_ref[...], hl_ref[...])
    z = _dot(h, win_ref[...])
    c, s1, s2 = c_ref[...], s1_ref[...], s2_ref[...]

    ckvn = _rms(z[:, Q_LORA:Q_LORA + KV_LORA], kvg_ref[...]).astype(BF16)
    kpe = _rope_tile(z[:, Q_LORA + KV_LORA:], c, s1, s2, False).astype(BF16)
    kn = _dot(ckvn, wuk_ref[...])
    v_ref[...] = lax.dot_general(wuv_ref[...], ckvn, (((1,), (1,)), ((), ())),
                                 preferred_element_type=F32).astype(BF16)
    for hd in range(MLA_HEADS):
        k_ref[:, hd * HEAD_PAD:hd * HEAD_PAD + QK_NOPE] = kn[:, hd * QK_NOPE:(hd + 1) * QK_NOPE].astype(BF16)
        k_ref[:, hd * HEAD_PAD + QK_NOPE:(hd + 1) * HEAD_PAD] = kpe

    @pl.when(jnp.logical_not(is_ctx))
    def _():
        cqn = _rms(z[:, :Q_LORA], qg_ref[...]).astype(BF16)
        qn = _dot(cqn, wqn_ref[...])
        qp = _dot(cqn, wqp_ref[...])
        for hd in range(MLA_HEADS):
            q_ref[:, hd * HEAD_PAD:hd * HEAD_PAD + QK_NOPE] = (
                qn[:, hd * QK_NOPE:(hd + 1) * QK_NOPE] * q_scale).astype(BF16)
            xt = qp[:, (hd // 2) * LANES:(hd // 2 + 1) * LANES]
            q_ref[:, hd * HEAD_PAD + QK_NOPE:(hd + 1) * HEAD_PAD] = (
                _rope_tile(xt, c, s1, s2, hd % 2 == 1) * q_scale).astype(BF16)


def _mla_proj(hl, hc, p, tabs):
    bsz, seq, d = hl.shape
    lc = hc.shape[1]
    tm = MLA_TM
    assert lc == tm
    nlt = seq // tm
    nt = nlt + 1
    kern = functools.partial(_mla_proj_kernel, n_lat_tiles=nlt,
                             q_scale=float((QK_NOPE + QK_ROPE) ** -0.5 * LOG2_E))
    tab = pl.BlockSpec((tm, LANES), lambda b, t: (t, 0))
    hw = MLA_HEADS * HEAD_PAD
    return pl.pallas_call(
        kern,
        grid=(bsz, nt),
        in_specs=[
            pl.BlockSpec((None, tm, d), lambda b, t: (b, jnp.minimum(t, nlt - 1), 0)),
            pl.BlockSpec((None, tm, d), lambda b, t: (b, 0, 0)),
            _const_spec((d, _ZPAD)),
            pl.BlockSpec((1, Q_LORA), lambda b, t: (0, 0)),
            pl.BlockSpec((1, KV_LORA), lambda b, t: (0, 0)),
            _const_spec((Q_LORA, MLA_HEADS * QK_NOPE)),
            _const_spec((Q_LORA, MLA_HEADS * QK_ROPE)),
            _const_spec((KV_LORA, MLA_HEADS * QK_NOPE)),
            _const_spec((MLA_HEADS * V_DIM, KV_LORA)),
            tab, tab, tab,
        ],
        out_specs=[
            pl.BlockSpec((None, tm, hw), lambda b, t: (b, jnp.minimum(t, nlt - 1), 0)),
            pl.BlockSpec((None, tm, hw), lambda b, t: (b, t, 0)),
            pl.BlockSpec((None, MLA_HEADS * V_DIM, tm), lambda b, t: (b, 0, t)),
        ],
        out_shape=[
            jax.ShapeDtypeStruct((bsz, seq, hw), BF16),
            jax.ShapeDtypeStruct((bsz, seq + lc, hw), BF16),
            jax.ShapeDtypeStruct((bsz, MLA_HEADS * V_DIM, seq + lc), BF16),
        ],
        compiler_params=_cparams("parallel", "arbitrary"),
        name="mla_proj",
    )(hl, hc, p["w_in"], p["q_norm_g"], p["kv_norm_g"], p["w_uq_n"], p["w_uq_p"],
      p["w_uk"], p["w_uv"], *tabs)


def _key_chunks(lk, n):
    tiles = lk // HEAD_PAD
    assert tiles * HEAD_PAD == lk and tiles >= n
    sizes = [(tiles // n + (1 if c >= n - tiles % n else 0)) * HEAD_PAD for c in range(n)]
    starts = [sum(sizes[:c]) for c in range(n)]
    return list(zip(starts, sizes))


def _attn_kernel(q_ref, k_ref, vt_ref, o_ref, *, chunks, heads):
    items = [(j, c) for j in range(heads) for c in range(len(chunks))]

    def scores(j, c):
        start, size = chunks[c]
        return lax.dot_general(k_ref[start:start + size, j * HEAD_PAD:(j + 1) * HEAD_PAD],
                               q_ref[:, j * HEAD_PAD:(j + 1) * HEAD_PAD], (((1,), (1,)), ((), ())),
                               preferred_element_type=F32)

    m = l = acc = None
    s_next = scores(*items[0])
    for n, (j, c) in enumerate(items):
        start, size = chunks[c]
        s = s_next
        if n + 1 < len(items):
            s_next = scores(*items[n + 1])
        mc = jnp.max(s, axis=0, keepdims=True)
        m_new = mc if c == 0 else jnp.maximum(m, mc)
        p = jnp.exp2(s - m_new)
        lc = jnp.sum(p, axis=0, keepdims=True)
        pv = _dot(vt_ref[j * V_DIM:(j + 1) * V_DIM, start:start + size], p.astype(BF16))
        if c == 0:
            l, acc = lc, pv
        else:
            alpha = jnp.exp2(m - m_new)
            l = alpha * l + lc
            acc = alpha * acc + pv
        m = m_new
        if c == len(chunks) - 1:
            o_ref[:, j * V_DIM:(j + 1) * V_DIM] = (acc / l).T.astype(o_ref.dtype)


def _attention(q, k, v, *, tq, n_chunks, heads):
    bsz, lq, _ = q.shape
    lk = k.shape[1]
    return pl.pallas_call(
        functools.partial(_attn_kernel, chunks=_key_chunks(lk, n_chunks), heads=heads),
        grid=(bsz, MLA_HEADS // heads, lq // tq),
        in_specs=[
            pl.BlockSpec((None, tq, heads * HEAD_PAD), lambda b, h, i: (b, i, h)),
            pl.BlockSpec((None, lk, heads * HEAD_PAD), lambda b, h, i: (b, 0, h)),
            pl.BlockSpec((None, heads * V_DIM, lk), lambda b, h, i: (b, h, 0)),
        ],
        out_specs=pl.BlockSpec((None, tq, heads * V_DIM), lambda b, h, i: (b, i, h)),
        out_shape=jax.ShapeDtypeStruct((bsz, lq, MLA_HEADS * V_DIM), BF16),
        compiler_params=_cparams("parallel", "parallel", "arbitrary"),
        name="mla_attention",
    )(q, k, v)


def _oproj_kernel(a_ref, x_ref, gate_ref, w_ref, ng_ref, sh_ref, sc_ref, *rest, carry):
    if carry:
        cast_src_ref, o_ref, hn_ref, cast_dst_ref = rest
        _carry_cast(cast_src_ref, cast_dst_ref)
    else:
        o_ref, hn_ref = rest
    y = x_ref[...] + gate_ref[...] * _dot(a_ref[...], w_ref[...])
    o_ref[...] = y
    hn_ref[...] = _norm_mod(y, ng_ref[...], sh_ref[...], sc_ref[...]).astype(BF16)


def _oproj(a, x, gate, w, nxt, *, tm, carry=None):
    m, d = x.shape
    kdim = a.shape[1]
    tiles_per_batch = m // gate.shape[0] // tm
    tile = pl.BlockSpec((tm, d), lambda i: (i, 0))
    c_in, c_out, c_shape, c_args = _carry(carry, m // tm)
    return pl.pallas_call(
        functools.partial(_oproj_kernel, carry=carry is not None),
        grid=(m // tm,),
        in_specs=[
            pl.BlockSpec((tm, kdim), lambda i: (i, 0)),
            tile,
            pl.BlockSpec((None, 1, d), lambda i: (i // tiles_per_batch, 0, 0)),
            _const_spec((kdim, d)),
            *_next_specs(d, tiles_per_batch),
            *c_in,
        ],
        out_specs=[tile, tile, *c_out],
        out_shape=[jax.ShapeDtypeStruct((m, d), F32), jax.ShapeDtypeStruct((m, d), BF16), *c_shape],
        compiler_params=_cparams("parallel"),
        name="attn_out_proj",
    )(a, x, gate, w, *nxt, *c_args)


def _rope_tables(seq, ctx_len):
    rows = seq // GRID_W
    row = jnp.repeat(jnp.arange(rows, dtype=F32), GRID_W)
    col = jnp.tile(jnp.arange(GRID_W, dtype=F32), rows)
    n_freq = QK_ROPE // 4
    inv = ROPE_THETA ** (-jnp.arange(n_freq, dtype=F32) / n_freq)
    ang = jnp.concatenate([row[:, None] * inv, col[:, None] * inv], axis=-1)
    cos, sin = jnp.cos(ang), jnp.sin(ang)
    half = QK_ROPE // 2
    cos = jnp.concatenate([cos, jnp.ones((ctx_len, half), F32)], axis=0)
    sin = jnp.concatenate([sin, jnp.zeros((ctx_len, half), F32)], axis=0)
    zh = jnp.zeros_like(cos)
    c = jnp.concatenate([cos, cos, zh, zh], axis=-1)
    s1 = jnp.concatenate([-sin, zh, zh, zh], axis=-1)
    s2 = jnp.concatenate([zh, sin, zh, zh], axis=-1)
    return c, s1, s2


def kernel(x, c, ctx, c_ctx, norm1_g, norm2_g, w_ada, b_ada, ab_w_in, ab_b_in, a_ln_g, a_ln_b, a_w_s, a_b_s, b_conv_w, b_conv_b, b_ln_g, b_ln_b, ab_w_out, mla_w_in, mla_q_norm_g, mla_w_uq, mla_kv_norm_g, mla_w_ukv, mla_w_o, ffn_w_up, ffn_conv_w, ffn_conv_b, ffn_w_down, final_norm_g):
    bsz, seq, d = x.shape
    lc = ctx.shape[1]
    depth = w_ada.shape[0]
    assert depth == 2, "the call chain below is written for one even (mixer) layer followed by one MLA layer"

    cv = jnp.concatenate([c, c_ctx[None, :], jnp.zeros((8 - bsz - 1, d), F32)], axis=0)
    mods = _adaln(cv, w_ada, b_ada)

    def lat_mod(i, k):
        return mods[i, :bsz, k * d:(k + 1) * d].reshape(bsz, 1, d)

    def ctx_mod(i, k):
        return mods[i, bsz:bsz + 1, k * d:(k + 1) * d].reshape(1, 1, d)

    ffn_w = (ffn_w_up, ffn_conv_w, ffn_conv_b.reshape(depth, 1, -1))
    xl = x.reshape(bsz * seq, d)
    xc = ctx.reshape(bsz * lc, d)
    n_ctx_rows = bsz * lc

    p = {
        "a_ln_g": a_ln_g[0].reshape(1, -1), "a_ln_b": a_ln_b[0].reshape(1, -1),
        "w_s": a_w_s[0].astype(BF16),
        "b_s_full": jnp.repeat(a_b_s[0].T, A_WIDTH // A_HEADS, axis=1),
        "conv_w": b_conv_w[0], "conv_b": b_conv_b[0].reshape(1, -1),
        "b_ln_g": b_ln_g[0].reshape(1, -1), "b_ln_b": b_ln_b[0].reshape(1, -1),
        "w_out": ab_w_out[0].astype(BF16),
    }
    w_in = ab_w_in[0].astype(BF16)
    b_in = ab_b_in[0].reshape(1, -1)
    ng1, ng2 = norm1_g[0].reshape(1, d), norm2_g[0].reshape(1, d)
    ng1_next = norm1_g[1].reshape(1, d)

    z = _nmm(xl, ng1, lat_mod(0, 0), lat_mod(0, 1), w_in, b_in, tm=NMM_TM, tn=NMM_TN)
    xl, hl, w_down0 = _abmix(z, xl, lat_mod(0, 2), p, (ng2, lat_mod(0, 3), lat_mod(0, 4)), tm=ABMIX_TM, seg_len=seq,
                             carry=(ffn_w_down, 0))
    xl, hl = _ffn(hl, xl, lat_mod(0, 5), *ffn_w, w_down0, (ng1_next, lat_mod(1, 0), lat_mod(1, 1)), layer=0,
                  tm_up=FFN_UP_TM, tf=FFN_TF, tm_down=FFN_DOWN_TM, seg_len=seq, final=False)

    zc = _nmm(xc, ng1, ctx_mod(0, 0), ctx_mod(0, 1), w_in, b_in, tm=n_ctx_rows, tn=NMM_TN)
    xc, hc = _abmix(zc, xc, ctx_mod(0, 2), p, (ng2, ctx_mod(0, 3), ctx_mod(0, 4)), tm=CTX_TM, seg_len=lc)
    xc, hc = _ffn(hc, xc, ctx_mod(0, 5), *ffn_w, w_down0, (ng1_next, ctx_mod(1, 0), ctx_mod(1, 1)), layer=0,
                  tm_up=n_ctx_rows, tf=FFN_TF, tm_down=CTX_TM, seg_len=lc, final=False)

    w_uq = mla_w_uq[0].reshape(Q_LORA, MLA_HEADS, QK_NOPE + QK_ROPE)
    w_ukv = mla_w_ukv[0].reshape(KV_LORA, MLA_HEADS, QK_NOPE + V_DIM)
    p = {
        "w_in": jnp.pad(mla_w_in[0], ((0, 0), (0, LANES - QK_ROPE))).astype(BF16),
        "q_norm_g": mla_q_norm_g[0].reshape(1, -1), "kv_norm_g": mla_kv_norm_g[0].reshape(1, -1),
        "w_uq_n": w_uq[:, :, :QK_NOPE].reshape(Q_LORA, -1).astype(BF16),
        "w_uq_p": w_uq[:, :, QK_NOPE:].reshape(Q_LORA, -1).astype(BF16),
        "w_uk": w_ukv[:, :, :QK_NOPE].reshape(KV_LORA, -1).astype(BF16),
        "w_uv": w_ukv[:, :, QK_NOPE:].reshape(KV_LORA, -1).T.astype(BF16),
    }
    q, k, v = _mla_proj(hl.reshape(bsz, seq, d), hc.reshape(bsz, lc, d), p, _rope_tables(seq, lc))
    a = _attention(q, k, v, tq=ATTN_TQ, n_chunks=ATTN_KEY_CHUNKS, heads=ATTN_HEADS_PER_STEP)
    xl, hl, w_down1 = _oproj(a.reshape(bsz * seq, -1), xl, lat_mod(1, 2), mla_w_o[0].astype(BF16),
                             (norm2_g[1].reshape(1, d), lat_mod(1, 3), lat_mod(1, 4)), tm=OPROJ_TM,
                             carry=(ffn_w_down, 1))
    out = _ffn(hl, xl, lat_mod(1, 5), *ffn_w, w_down1, (final_norm_g.reshape(1, d),), layer=1,
               tm_up=FFN_UP_TM, tf=FFN_TF, tm_down=FFN_DOWN_TM, seg_len=seq, final=True)
    return out.reshape(bsz, seq, d)
```
